```python
import math
import jax, jax.numpy as jnp
from jax import lax
import numpy as np

D_MODEL = 1024
BATCH = 8
SEQ = 2048
DEPTH = 1
DEC_BATCH = 32
DEC_SEQ = 1
PAST_LEN = 16384
PAGE_SIZE = 128

HEAD_DIM = 64
DIFF_HEADS = 4
DIFF_WIDTH = DIFF_HEADS * 2 * HEAD_DIM
FOX_HEADS = 8
FOX_WIDTH = FOX_HEADS * HEAD_DIM
MIX_WIDTH = DIFF_WIDTH + FOX_WIDTH
IN_COLS = 3 * DIFF_WIDTH + 3 * FOX_WIDTH + FOX_HEADS
ROPE_THETA = 10000.0
Q_BLOCK = 128
N_GROUPS = 4
EXPERTS_PER_GROUP = 4
N_EXPERTS = N_GROUPS * EXPERTS_PER_GROUP
TOP_K_INNER = 2
D_EXPERT = 512
LN_EPS = 1e-5
SUBLN_EPS = 1e-5
DN_ALPHA = (2 * DEPTH) ** 0.25
DN_BETA = (8 * DEPTH) ** -0.25
FORGET_BIAS_INIT = 5.0

kernel_name = 'hymba_diff_fox_hmoe_step'


def layer_norm(x, g, b):
    xf = x.astype(jnp.float32)
    mu = jnp.mean(xf, axis=-1, keepdims=True)
    var = jnp.mean(jnp.square(xf - mu), axis=-1, keepdims=True)
    return ((xf - mu) * lax.rsqrt(var + LN_EPS) * g.astype(jnp.float32) + b.astype(jnp.float32)).astype(x.dtype)


def rms_norm(x, g):
    xf = x.astype(jnp.float32)
    ms = jnp.mean(jnp.square(xf), axis=-1, keepdims=True)
    return (xf * lax.rsqrt(ms + SUBLN_EPS) * g.astype(jnp.float32)).astype(x.dtype)


def rope(x, pos):
    half = HEAD_DIM // 2
    inv = ROPE_THETA ** (-jnp.arange(half, dtype=jnp.float32) * 2.0 / HEAD_DIM)
    ang = pos.astype(jnp.float32)[:, None] * inv[None, :]
    shape = (ang.shape[0],) + (1,) * (x.ndim - 3) + (half,)
    cos = jnp.cos(ang).reshape(shape)
    sin = jnp.sin(ang).reshape(shape)
    xf = x.astype(jnp.float32)
    x1, x2 = xf[..., :half], xf[..., half:]
    return jnp.concatenate([x1 * cos - x2 * sin, x2 * cos + x1 * sin], axis=-1).astype(x.dtype)


def diff_lambda(lq1, lk1, lq2, lk2, lam_init):
    f = jnp.float32
    return (jnp.exp(jnp.sum(lq1.astype(f) * lk1.astype(f)))
            - jnp.exp(jnp.sum(lq2.astype(f) * lk2.astype(f))) + lam_init)


def project(x, pos, w_in, b_forget):
    B, T, _ = x.shape
    p = jnp.einsum('btd,dc->btc', x, w_in)
    dw, fw = DIFF_WIDTH, FOX_WIDTH
    dq = p[..., 0:dw].reshape(B, T, DIFF_HEADS, 2, HEAD_DIM)
    dk = p[..., dw:2 * dw].reshape(B, T, DIFF_HEADS, 2, HEAD_DIM)
    dv = p[..., 2 * dw:3 * dw].reshape(B, T, DIFF_HEADS, 2 * HEAD_DIM)
    f0 = 3 * dw
    fq = p[..., f0:f0 + fw].reshape(B, T, FOX_HEADS, HEAD_DIM)
    fk = p[..., f0 + fw:f0 + 2 * fw].reshape(B, T, FOX_HEADS, HEAD_DIM)
    fv = p[..., f0 + 2 * fw:f0 + 3 * fw].reshape(B, T, FOX_HEADS, HEAD_DIM)
    fz = p[..., f0 + 3 * fw:]
    dq = rope(dq, pos).reshape(B, T, DIFF_HEADS, 2 * HEAD_DIM)
    dk = rope(dk, pos).reshape(B, T, DIFF_HEADS, 2 * HEAD_DIM)
    logf = jax.nn.log_sigmoid(fz.astype(jnp.float32) + b_forget.astype(jnp.float32))
    return dq, dk, dv, fq, fk, fv, logf


def attend(dq, fq, cq, q_pos, segments, lam):
    scale = HEAD_DIM ** -0.5
    s1, s2, sf = [], [], []
    for dk, dv, fk, fv, ck, k_pos in segments:
        mask = (k_pos[None, :] <= q_pos[:, None])[None, None]
        a1 = jnp.einsum('bqhd,bkhd->bhqk', dq[..., :HEAD_DIM], dk[..., :HEAD_DIM]).astype(jnp.float32) * scale
        a2 = jnp.einsum('bqhd,bkhd->bhqk', dq[..., HEAD_DIM:], dk[..., HEAD_DIM:]).astype(jnp.float32) * scale
        bias = jnp.swapaxes(cq, 1, 2)[..., :, None] - jnp.swapaxes(ck.astype(jnp.float32), 1, 2)[..., None, :]
        af = jnp.einsum('bqhd,bkhd->bhqk', fq, fk).astype(jnp.float32) * scale + bias
        s1.append(jnp.where(mask, a1, -jnp.inf))
        s2.append(jnp.where(mask, a2, -jnp.inf))
        sf.append(jnp.where(mask, af, -jnp.inf))
    p_diff = (jax.nn.softmax(jnp.concatenate(s1, axis=-1), axis=-1)
              - lam * jax.nn.softmax(jnp.concatenate(s2, axis=-1), axis=-1))
    p_fox = jax.nn.softmax(jnp.concatenate(sf, axis=-1), axis=-1)
    od, of, off = 0.0, 0.0, 0
    for dk, dv, fk, fv, ck, k_pos in segments:
        n = dk.shape[1]
        od = od + jnp.einsum('bhqk,bkhe->bqhe', p_diff[..., off:off + n].astype(dv.dtype), dv)
        of = of + jnp.einsum('bhqk,bkhe->bqhe', p_fox[..., off:off + n].astype(fv.dtype), fv)
        off += n
    return od, of


def hier_moe(x, w_rg, b_rg, w_re, b_re, w_gate, w_up, w_down):
    B, T, D = x.shape
    xt = x.reshape(B * T, D)
    g_prob = jax.nn.softmax(jnp.einsum('nd,dg->ng', xt, w_rg).astype(jnp.float32) + b_rg.astype(jnp.float32), axis=-1)
    g_val, g_idx = lax.top_k(g_prob, 1)
    e_logits = (jnp.einsum('nd,de->ne', xt, w_re).astype(jnp.float32) + b_re.astype(jnp.float32))
    e_logits = e_logits.reshape(-1, N_GROUPS, EXPERTS_PER_GROUP)
    e_in = jnp.take_along_axis(e_logits, g_idx[:, :, None], axis=1)[:, 0]
    e_val, e_idx = lax.top_k(e_in, TOP_K_INNER)
    e_w = jax.nn.softmax(e_val, axis=-1) * g_val
    expert_id = g_idx * EXPERTS_PER_GROUP + e_idx
    gates = jnp.sum(jax.nn.one_hot(expert_id, N_EXPERTS, dtype=jnp.float32) * e_w[..., None], axis=1)
    h = jax.nn.silu(jnp.einsum('nd,edf->nef', xt, w_gate)) * jnp.einsum('nd,edf->nef', xt, w_up)
    h = h * gates[..., None].astype(h.dtype)
    return jnp.einsum('nef,efd->nd', h, w_down).reshape(B, T, D)


def trunk_tail(x, od, of, lam_init, subln_g, w_out, ln1_g, ln1_b, w_rg, b_rg, w_re, b_re, w_gate, w_up, w_down, ln2_g, ln2_b):
    B, T, _ = x.shape
    od = rms_norm(od, subln_g) * (1.0 - lam_init)
    mix = jnp.concatenate([od.reshape(B, T, DIFF_WIDTH), of.reshape(B, T, FOX_WIDTH).astype(od.dtype)], axis=-1)
    h = layer_norm(DN_ALPHA * x + jnp.einsum('btc,cd->btd', mix, w_out), ln1_g, ln1_b)
    return layer_norm(DN_ALPHA * h + hier_moe(h, w_rg, b_rg, w_re, b_re, w_gate, w_up, w_down), ln2_g, ln2_b)


def setup_inputs(seed: int = 0) -> dict:
    key = jax.random.key(seed)
    ks = jax.random.split(key, 32)
    n_pages = PAST_LEN // PAGE_SIZE
    n_used = DEC_BATCH * n_pages
    n_pool = n_used + max(1, n_used // 4)
    f = jnp.float32
    nrm = lambda k, shape: jax.random.normal(k, shape, dtype=f)
    col_scale = jnp.concatenate([
        jnp.ones((2 * DIFF_WIDTH,), f), DN_BETA * jnp.ones((DIFF_WIDTH,), f),
        jnp.ones((2 * FOX_WIDTH,), f), DN_BETA * jnp.ones((FOX_WIDTH,), f),
        jnp.ones((FOX_HEADS,), f)]) * D_MODEL ** -0.5
    page_table = jax.random.permutation(ks[0], n_pool)[:n_used].reshape(DEC_BATCH, n_pages).astype(jnp.int32)
    return {
        'x_prompt': nrm(ks[1], (BATCH, SEQ, D_MODEL)),
        'x_sample': nrm(ks[2], (DEC_BATCH, DEC_SEQ, D_MODEL)),
        'cache_diff_k': nrm(ks[3], (DEPTH, n_pool, PAGE_SIZE, DIFF_HEADS, 2 * HEAD_DIM)),
        'cache_diff_v': nrm(ks[4], (DEPTH, n_pool, PAGE_SIZE, DIFF_HEADS, 2 * HEAD_DIM)) * DN_BETA,
        'cache_fox_k': nrm(ks[5], (DEPTH, n_pool, PAGE_SIZE, FOX_HEADS, HEAD_DIM)),
        'cache_fox_v': nrm(ks[6], (DEPTH, n_pool, PAGE_SIZE, FOX_HEADS, HEAD_DIM)) * DN_BETA,
        'cache_fox_logf': jax.nn.log_sigmoid(FORGET_BIAS_INIT + nrm(ks[7], (DEPTH, n_pool, PAGE_SIZE, FOX_HEADS))),
        'page_table': page_table,
        'w_in': nrm(ks[8], (DEPTH, D_MODEL, IN_COLS)) * col_scale,
        'b_forget': FORGET_BIAS_INIT + 0.1 * nrm(ks[9], (DEPTH, FOX_HEADS)),
        'lambda_q1': 0.1 * nrm(ks[10], (DEPTH, HEAD_DIM)),
        'lambda_k1': 0.1 * nrm(ks[11], (DEPTH, HEAD_DIM)),
        'lambda_q2': 0.1 * nrm(ks[12], (DEPTH, HEAD_DIM)),
        'lambda_k2': 0.1 * nrm(ks[13], (DEPTH, HEAD_DIM)),
        'subln_g': 1.0 + 0.02 * nrm(ks[14], (DEPTH, 2 * HEAD_DIM)),
        'w_out': nrm(ks[15], (DEPTH, MIX_WIDTH, D_MODEL)) * (MIX_WIDTH ** -0.5 * DN_BETA),
        'ln1_g': 1.0 + 0.02 * nrm(ks[16], (DEPTH, D_MODEL)),
        'ln1_b': 0.02 * nrm(ks[17], (DEPTH, D_MODEL)),
        'w_router_group': nrm(ks[18], (DEPTH, D_MODEL, N_GROUPS)) * D_MODEL ** -0.5,
        'b_router_group': 0.01 * nrm(ks[19], (DEPTH, N_GROUPS)),
        'w_router_expert': nrm(ks[20], (DEPTH, D_MODEL, N_EXPERTS)) * D_MODEL ** -0.5,
        'b_router_expert': 0.01 * nrm(ks[21], (DEPTH, N_EXPERTS)),
        'w_gate': nrm(ks[22], (DEPTH, N_EXPERTS, D_MODEL, D_EXPERT)) * D_MODEL ** -0.5,
        'w_up': nrm(ks[23], (DEPTH, N_EXPERTS, D_MODEL, D_EXPERT)) * (D_MODEL ** -0.5 * DN_BETA),
        'w_down': nrm(ks[24], (DEPTH, N_EXPERTS, D_EXPERT, D_MODEL)) * (D_EXPERT ** -0.5 * DN_BETA),
        'ln2_g': 1.0 + 0.02 * nrm(ks[25], (DEPTH, D_MODEL)),
        'ln2_b': 0.02 * nrm(ks[26], (DEPTH, D_MODEL)),
    }


def reference(x_prompt, x_sample, cache_diff_k, cache_diff_v, cache_fox_k, cache_fox_v, cache_fox_logf, page_table,
              w_in, b_forget, lambda_q1, lambda_k1, lambda_q2, lambda_k2, subln_g, w_out, ln1_g, ln1_b,
              w_router_group, b_router_group, w_router_expert, b_router_expert, w_gate, w_up, w_down, ln2_g, ln2_b):
    B, S, _ = x_prompt.shape
    DB, T, _ = x_sample.shape
    n_pages = page_table.shape[1]
    past_len = n_pages * cache_diff_k.shape[2]
    p_pos = jnp.arange(S, dtype=jnp.int32)
    s_pos = past_len + jnp.arange(T, dtype=jnp.int32)
    past_pos = jnp.arange(past_len, dtype=jnp.int32)
    xp, xs = x_prompt, x_sample
    pst = [[] for _ in range(5)]
    sst = [[] for _ in range(5)]
    for l in range(DEPTH):
        lam_init = 0.8 - 0.6 * math.exp(-0.3 * l)
        lam = diff_lambda(lambda_q1[l], lambda_k1[l], lambda_q2[l], lambda_k2[l], lam_init)
        tail = lambda x, od, of: trunk_tail(x, od, of, lam_init, subln_g[l], w_out[l], ln1_g[l], ln1_b[l],
                                            w_router_group[l], b_router_group[l], w_router_expert[l],
                                            b_router_expert[l], w_gate[l], w_up[l], w_down[l], ln2_g[l], ln2_b[l])
        dq, dk, dv, fq, fk, fv, logf = project(xp, p_pos, w_in[l], b_forget[l])
        c = jnp.cumsum(logf, axis=1)

        def block(i):
            start = i * Q_BLOCK
            sl = lambda a: lax.dynamic_slice_in_dim(a, start, Q_BLOCK, axis=1)
            qp = start + jnp.arange(Q_BLOCK, dtype=jnp.int32)
            return attend(sl(dq), sl(fq), sl(c), qp, [(dk, dv, fk, fv, c, p_pos)], lam)

        od, of = lax.map(block, jnp.arange(S // Q_BLOCK))
        od = jnp.swapaxes(od, 0, 1).reshape(B, S, DIFF_HEADS, 2 * HEAD_DIM)
        of = jnp.swapaxes(of, 0, 1).reshape(B, S, FOX_HEADS, HEAD_DIM)
        for lst, a in zip(pst, (dk, dv, fk, fv, logf)):
            lst.append(a)
        xp_next = tail(xp, od, of)
        sdq, sdk, sdv, sfq, sfk, sfv, slogf = project(xs, s_pos, w_in[l], b_forget[l])
        gather = lambda cache: cache[l][page_table].reshape((DB, past_len) + cache.shape[3:])
        plogf = gather(cache_fox_logf).astype(jnp.float32)
        c_past = jnp.cumsum(plogf, axis=1)
        c_new = c_past[:, -1:, :] + jnp.cumsum(slogf, axis=1)
        segments = [(gather(cache_diff_k), gather(cache_diff_v), gather(cache_fox_k), gather(cache_fox_v), c_past, past_pos),
                    (sdk, sdv, sfk, sfv, c_new, s_pos)]
        od_s, of_s = attend(sdq, sfq, c_new, s_pos, segments, lam)
        for lst, a in zip(sst, (sdk, sdv, sfk, sfv, slogf)):
            lst.append(a)
        xs_next = tail(xs, od_s, of_s)
        xp, xs = xp_next, xs_next
    p_dk, p_dv, p_fk, p_fv, p_lf = [jnp.stack(a, axis=0) for a in pst]
    s_dk, s_dv, s_fk, s_fv, s_lf = [jnp.stack(a, axis=0) for a in sst]
    return (xp, xs, p_dk, p_dv, p_fk, p_fv, p_lf, s_dk, s_dv, s_fk, s_fv, s_lf)
```

```python
import functools
import math

import jax
import jax.numpy as jnp
from jax import lax
from jax.experimental import pallas as pl
from jax.experimental.pallas import tpu as pltpu

F32 = jnp.float32
BF16 = jnp.bfloat16

HEAD_DIM = 64
DIFF_HEADS = 4
FOX_HEADS = 8
WIDTH = 512
D_MODEL = 1024
N_GROUPS = 4
EXPERTS_PER_GROUP = 4
N_EXPERTS = 16
D_EXPERT = 512
ROPE_THETA = 10000.0
LN_EPS = 1e-5
SUBLN_EPS = 1e-5
LANES = 128
NEG = -1e30
V7X_VMEM_LIMIT = 48 * 1024 * 1024


def _cparams(sem):
    return pltpu.CompilerParams(dimension_semantics=sem, vmem_limit_bytes=V7X_VMEM_LIMIT)


def _dot(a, b):
    return jnp.dot(a, b, preferred_element_type=F32)


def _dot_nt(a, b):
    return lax.dot_general(a, b, (((1,), (1,)), ((), ())), preferred_element_type=F32)


def _dot_hi(a, b):
    return jnp.dot(a, b, preferred_element_type=F32, precision=lax.Precision.HIGHEST)


def _layer_norm(x, g, b):
    mu = jnp.mean(x, axis=-1, keepdims=True)
    xc = x - mu
    var = jnp.mean(xc * xc, axis=-1, keepdims=True)
    return xc * lax.rsqrt(var + LN_EPS) * g + b


def _proj_kernel(x_ref, w_ref, cos_ref, sin_ref, bf_ref,
                 dq_ref, dkb_ref, dvb_ref, fq_ref, fkb_ref, fvb_ref,
                 dk_ref, dv_ref, fk_ref, fv_ref, logf_ref):
    tm = x_ref.shape[0]
    xb = x_ref[...].astype(BF16)
    cos = cos_ref[...]
    sin = sin_ref[...]
    lane = lax.broadcasted_iota(jnp.int32, (tm, WIDTH), 1)
    first_half = (lane & (HEAD_DIM // 2)) == 0
    scale = HEAD_DIM ** -0.5

    def mm(c):
        return _dot(xb, w_ref[:, c * WIDTH:(c + 1) * WIDTH])

    def rope(p):
        partner = jnp.where(first_half,
                            pltpu.roll(p, WIDTH - HEAD_DIM // 2, 1),
                            pltpu.roll(p, HEAD_DIM // 2, 1))
        return p * cos + partner * sin

    dq = rope(mm(0))
    dq_ref[...] = (dq * scale).astype(BF16)
    dk = rope(mm(1))
    dk_ref[...] = dk
    dkb_ref[...] = dk.astype(BF16)
    dv = mm(2)
    dv_ref[...] = dv
    dvb_ref[...] = dv.astype(BF16)
    fq_ref[...] = (mm(3) * scale).astype(BF16)
    fk = mm(4)
    fk_ref[...] = fk
    fkb_ref[...] = fk.astype(BF16)
    fv = mm(5)
    fv_ref[...] = fv
    fvb_ref[...] = fv.astype(BF16)
    z = _dot(xb, w_ref[:, 6 * WIDTH:6 * WIDTH + LANES]) + bf_ref[...]
    logf = jnp.minimum(z, 0.0) - jnp.log1p(jnp.exp(-jnp.abs(z)))
    logf_ref[...] = logf[:, :FOX_HEADS]


def _project(x2d, w_bf, cos_t, sin_t, bf_pad, tm, table_blocks):
    n = x2d.shape[0]
    grid = (n // tm,)
    row = lambda i: (i, 0)
    tab = lambda i: (i % table_blocks, 0)
    const = lambda i: (0, 0)
    wide = pl.BlockSpec((tm, WIDTH), row)
    out_shape = ([jax.ShapeDtypeStruct((n, WIDTH), BF16)] * 6
                 + [jax.ShapeDtypeStruct((n, WIDTH), F32)] * 4
                 + [jax.ShapeDtypeStruct((n, FOX_HEADS), F32)])
    return pl.pallas_call(
        _proj_kernel,
        grid=grid,
        in_specs=[pl.BlockSpec((tm, D_MODEL), row),
                  pl.BlockSpec(w_bf.shape, const),
                  pl.BlockSpec((tm, WIDTH), tab),
                  pl.BlockSpec((tm, WIDTH), tab),
                  pl.BlockSpec((1, LANES), const)],
        out_specs=[wide] * 10 + [pl.BlockSpec((tm, FOX_HEADS), row)],
        out_shape=out_shape,
        compiler_params=_cparams(("arbitrary",)),
        name="proj",
    )(x2d, w_bf, cos_t, sin_t, bf_pad)


def _cumsum_kernel(x_ref, o_ref):
    rows, s = x_ref.shape
    t = lax.broadcasted_iota(jnp.int32, (LANES, LANES), 0)
    u = lax.broadcasted_iota(jnp.int32, (LANES, LANES), 1)
    tri = (t <= u).astype(F32)
    carry = jnp.zeros((rows, 1), F32)
    for ch in range(s // LANES):
        blk = _dot_hi(x_ref[:, ch * LANES:(ch + 1) * LANES], tri) + carry
        o_ref[:, ch * LANES:(ch + 1) * LANES] = blk
        carry = blk[:, LANES - 1:LANES]


def _cumsum_rows(xt):
    return pl.pallas_call(
        _cumsum_kernel,
        out_shape=jax.ShapeDtypeStruct(xt.shape, F32),
        name="cumsum",
    )(xt)


def _attn_kernel(dq_ref, fq_ref, dk_ref, dv_ref, fk_ref, fv_ref, c_ref, ct_ref, lamv_ref, g_ref,
                 mix_ref, *, tq, lam_init):
    qi = pl.program_id(1)
    tk = tq
    lane = lax.broadcasted_iota(jnp.int32, (tq, LANES), 1)
    lo = lane < HEAD_DIM
    rowi = lax.broadcasted_iota(jnp.int32, (tq, tk), 0)
    coli = lax.broadcasted_iota(jnp.int32, (tq, tk), 1)
    causal = coli <= rowi

    lv = lamv_ref[...]
    lam = (jnp.exp(jnp.sum(lv[0:1] * lv[1:2], axis=1, keepdims=True))
           - jnp.exp(jnp.sum(lv[2:3] * lv[3:4], axis=1, keepdims=True)) + lam_init)

    def run_pair(q_pair, k_ref, v_ref, off, bias):
        qa = jnp.where(lo, q_pair, jnp.zeros_like(q_pair))
        qb = jnp.where(lo, jnp.zeros_like(q_pair), q_pair)

        def step(j, carry, masked):
            out = []
            k = k_ref[0, pl.ds(pl.multiple_of(j * tk, tk), tk), off:off + LANES]
            v = v_ref[0, pl.ds(pl.multiple_of(j * tk, tk), tk), off:off + LANES]
            for idx, q in enumerate((qa, qb)):
                m, l, acc = carry[idx]
                s = _dot_nt(q, k)
                if bias is not None:
                    cq, h0 = bias
                    s = s + (cq[idx] - ct_ref[0, j, h0 + idx:h0 + idx + 1, :])
                if masked:
                    s = jnp.where(causal, s, NEG)
                m_new = jnp.maximum(m, jnp.max(s, axis=1, keepdims=True))
                alpha = jnp.exp(m - m_new)
                e = jnp.exp(s - m_new)
                l = alpha * l + jnp.sum(e, axis=1, keepdims=True)
                acc = alpha * acc + _dot(e.astype(BF16), v)
                out.append((m_new, l, acc))
            return tuple(out)

        init = tuple((jnp.full((tq, 1), NEG, F32), jnp.zeros((tq, 1), F32), jnp.zeros((tq, LANES), F32))
                     for _ in range(2))
        carry = lax.fori_loop(0, qi, lambda j, c: step(j, c, False), init)
        (ma, la, acca), (mb, lb, accb) = step(qi, carry, True)
        return acca / la, accb / lb

    g = g_ref[...]
    for h in range(DIFF_HEADS):
        off = h * LANES
        o1, o2 = run_pair(dq_ref[0, :, off:off + LANES], dk_ref, dv_ref, off, None)
        o = o1 - lam * o2
        ms = jnp.mean(o * o, axis=1, keepdims=True)
        o = o * lax.rsqrt(ms + SUBLN_EPS) * g * (1.0 - lam_init)
        mix_ref[0, :, off:off + LANES] = o.astype(BF16)
    for hp in range(FOX_HEADS // 2):
        off = hp * LANES
        cq = (c_ref[0, :, 2 * hp:2 * hp + 1], c_ref[0, :, 2 * hp + 1:2 * hp + 2])
        oa, ob = run_pair(fq_ref[0, :, off:off + LANES], fk_ref, fv_ref, off, (cq, 2 * hp))
        mix_ref[0, :, WIDTH + off:WIDTH + off + LANES] = jnp.where(lo, oa, ob).astype(BF16)


def _prompt_attention(dq, fq, dk, dv, fk, fv, c, ct, lamv, g, tq, lam_init):
    b, s, _ = dq.shape
    qspec = pl.BlockSpec((1, tq, WIDTH), lambda bi, qi: (bi, qi, 0))
    kspec = pl.BlockSpec((1, s, WIDTH), lambda bi, qi: (bi, 0, 0))
    return pl.pallas_call(
        functools.partial(_attn_kernel, tq=tq, lam_init=lam_init),
        grid=(b, s // tq),
        in_specs=[qspec, qspec, kspec, kspec, kspec, kspec,
                  pl.BlockSpec((1, tq, FOX_HEADS), lambda bi, qi: (bi, qi, 0)),
                  pl.BlockSpec((1, s // tq, FOX_HEADS, tq), lambda bi, qi: (bi, 0, 0, 0)),
                  pl.BlockSpec((4, HEAD_DIM), lambda bi, qi: (0, 0)),
                  pl.BlockSpec((1, LANES), lambda bi, qi: (0, 0))],
        out_specs=pl.BlockSpec((1, tq, D_MODEL), lambda bi, qi: (bi, qi, 0)),
        out_shape=jax.ShapeDtypeStruct((b, s, D_MODEL), BF16),
        compiler_params=_cparams(("arbitrary", "arbitrary")),
        name="attn",
    )(dq, fq, dk, dv, fk, fv, c, ct, lamv, g)


def _suffix_kernel(x_ref, e_ref, t_ref):
    t = lax.broadcasted_iota(jnp.int32, (LANES, LANES), 0)
    u = lax.broadcasted_iota(jnp.int32, (LANES, LANES), 1)
    x = x_ref[...]
    e_ref[...] = _dot_hi(x, (t > u).astype(F32))
    t_ref[...] = _dot_hi(x, jnp.ones((LANES, LANES), F32))


def _page_suffix(lft, tr):
    n = lft.shape[0]
    spec = pl.BlockSpec((tr, LANES), lambda i: (i, 0))
    return pl.pallas_call(
        _suffix_kernel,
        grid=(n // tr,),
        in_specs=[spec],
        out_specs=[spec, spec],
        out_shape=[jax.ShapeDtypeStruct(lft.shape, F32)] * 2,
        compiler_params=_cparams(("arbitrary",)),
        name="suffix",
    )(lft)


def _decode_kernel(pt_ref, sdq_ref, sfq_ref, sdk_ref, sdv_ref, sfk_ref, sfv_ref, slogf_ref, lamv_ref, g_ref,
                   *rest, pages, lam_init):
    del pt_ref
    kd = rest[0:pages]
    vd = rest[pages:2 * pages]
    kf = rest[2 * pages:3 * pages]
    vf = rest[3 * pages:4 * pages]
    ex = rest[4 * pages:5 * pages]
    tot = rest[5 * pages:6 * pages]
    mix_ref = rest[6 * pages]
    md, ld, accd, mf, lf, accf, carry = rest[6 * pages + 1:]
    j = pl.program_id(1)
    rows = 2 * DIFF_HEADS

    @pl.when(j == 0)
    def _():
        md[...] = jnp.full(md.shape, NEG, F32)
        mf[...] = jnp.full(mf.shape, NEG, F32)
        ld[...] = jnp.zeros(ld.shape, F32)
        lf[...] = jnp.zeros(lf.shape, F32)
        accd[...] = jnp.zeros(accd.shape, F32)
        accf[...] = jnp.zeros(accf.shape, F32)
        carry[...] = jnp.zeros(carry.shape, F32)

    lane = lax.broadcasted_iota(jnp.int32, (rows, WIDTH), 1)
    r = lax.broadcasted_iota(jnp.int32, (rows, WIDTH), 0)
    half_mask = (lane >> 6) == r
    qd = jnp.where(half_mask, sdq_ref[0], 0.0)
    qf = jnp.where(half_mask, sfq_ref[0], 0.0)

    def update(m_ref, l_ref, acc_ref, s_list, v_refs):
        s = jnp.concatenate(s_list, axis=1)
        m_old = m_ref[...]
        m_new = jnp.maximum(m_old, jnp.max(s, axis=1, keepdims=True))
        alpha = jnp.exp(m_old - m_new)
        e = jnp.exp(s - m_new)
        l_ref[...] = alpha * l_ref[...] + jnp.sum(e, axis=1, keepdims=True)
        pv = _dot(e[:, 0:LANES], v_refs[0][...])
        for k in range(1, pages):
            pv = pv + _dot(e[:, k * LANES:(k + 1) * LANES], v_refs[k][...])
        acc_ref[...] = alpha * acc_ref[...] + pv
        m_ref[...] = m_new

    update(md, ld, accd, [_dot_nt(qd, kd[k][...]) for k in range(pages)], vd)

    run = carry[...]
    base = slogf_ref[0]
    sf = [None] * pages
    for k in reversed(range(pages)):
        sf[k] = _dot_nt(qf, kf[k][...]) + ((base + run) + ex[k][...])
        run = run + tot[k][...]
    carry[...] = run
    update(mf, lf, accf, sf, vf)

    @pl.when(j == pl.num_programs(1) - 1)
    def _():
        lv = lamv_ref[...]
        lam = (jnp.exp(jnp.sum(lv[0:1] * lv[1:2], axis=1, keepdims=True))
               - jnp.exp(jnp.sum(lv[2:3] * lv[3:4], axis=1, keepdims=True)) + lam_init)

        def finish(m_ref, l_ref, acc_ref, q8, k_new, v_new):
            s_new = jnp.sum(q8 * k_new, axis=1, keepdims=True)
            m_old = m_ref[...]
            m_fin = jnp.maximum(m_old, s_new)
            a = jnp.exp(m_old - m_fin)
            en = jnp.exp(s_new - m_fin)
            l_fin = a * l_ref[...] + en
            return (a * acc_ref[...] + en * v_new) / l_fin

        od8 = finish(md, ld, accd, qd, sdk_ref[0], sdv_ref[0])
        of8 = finish(mf, lf, accf, qf, sfk_ref[0], sfv_ref[0])
        head_mask = (lane >> 7) == (r >> 1)
        coef = jnp.where((r & 1) == 0, 1.0, -lam)
        od = jnp.sum(jnp.where(head_mask, od8 * coef, 0.0), axis=0, keepdims=True)
        seg = (lane >> 7) == r
        ms = jnp.sum(jnp.where(seg, od * od, 0.0), axis=1, keepdims=True) * (1.0 / LANES)
        rs = jnp.sum(jnp.where(seg, lax.rsqrt(ms + SUBLN_EPS), 0.0), axis=0, keepdims=True)
        od = od * rs * g_ref[...] * (1.0 - lam_init)
        of = jnp.sum(jnp.where(half_mask, of8, 0.0), axis=0, keepdims=True)
        mix_ref[0] = jnp.concatenate([od, of], axis=1)


def _decode_attention(page_table, sdq, sfq, sdk, sdv, sfk, sfv, slogf, lamv, g4,
                      ckd, cvd, ckf, cvf, ex, tot, pages, lam_init):
    db, n_pages = page_table.shape
    page = ckd.shape[1]
    n_steps = n_pages // pages

    def page_map(k):
        return lambda b, j, pt: (pt[b, (n_steps - 1 - j) * pages + k], 0, 0)

    vec = pl.BlockSpec((1, 1, WIDTH), lambda b, j, pt: (b, 0, 0))
    in_specs = [vec] * 6 + [
        pl.BlockSpec((1, FOX_HEADS, 1), lambda b, j, pt: (b, 0, 0)),
        pl.BlockSpec((4, HEAD_DIM), lambda b, j, pt: (0, 0)),
        pl.BlockSpec((1, WIDTH), lambda b, j, pt: (0, 0)),
    ]
    for _ in range(4):
        in_specs += [pl.BlockSpec((None, page, WIDTH), page_map(k)) for k in range(pages)]
    for _ in range(2):
        in_specs += [pl.BlockSpec((None, FOX_HEADS, LANES), page_map(k)) for k in range(pages)]
    rows = FOX_HEADS
    grid_spec = pltpu.PrefetchScalarGridSpec(
        num_scalar_prefetch=1,
        grid=(db, n_steps),
        in_specs=in_specs,
        out_specs=pl.BlockSpec((1, 1, D_MODEL), lambda b, j, pt: (b, 0, 0)),
        scratch_shapes=[pltpu.VMEM((rows, 1), F32), pltpu.VMEM((rows, 1), F32), pltpu.VMEM((rows, WIDTH), F32),
                        pltpu.VMEM((rows, 1), F32), pltpu.VMEM((rows, 1), F32), pltpu.VMEM((rows, WIDTH), F32),
                        pltpu.VMEM((rows, LANES), F32)],
    )
    args = ([sdq, sfq, sdk, sdv, sfk, sfv, slogf, lamv, g4]
            + [ckd] * pages + [cvd] * pages + [ckf] * pages + [cvf] * pages + [ex] * pages + [tot] * pages)
    return pl.pallas_call(
        functools.partial(_decode_kernel, pages=pages, lam_init=lam_init),
        grid_spec=grid_spec,
        out_shape=jax.ShapeDtypeStruct((db, 1, D_MODEL), F32),
        compiler_params=_cparams(("arbitrary", "arbitrary")),
        name="decode",
    )(page_table, *args)


def _tail_kernel(mix_ref, x_ref, wo_ref, g_ref, b_ref, wr_ref, br_ref, h_ref, hb_ref, gate_ref, *, alpha):
    y = _dot(mix_ref[...].astype(BF16), wo_ref[...])
    h = _layer_norm(alpha * x_ref[...] + y, g_ref[...], b_ref[...])
    h_ref[...] = h
    hb_ref[...] = h.astype(BF16)
    logits = _dot_hi(h, wr_ref[...]) + br_ref[...]
    tm = logits.shape[0]
    lane = lax.broadcasted_iota(jnp.int32, (tm, LANES), 1).astype(F32)
    big = float(LANES)
    gmask = lane < N_GROUPS
    gl = jnp.where(gmask, logits, NEG)
    gmax = jnp.max(gl, axis=1, keepdims=True)
    gsum = jnp.sum(jnp.where(gmask, jnp.exp(gl - gmax), 0.0), axis=1, keepdims=True)
    g_val = 1.0 / gsum
    g_idx = jnp.min(jnp.where(gmask & (gl == gmax), lane, big), axis=1, keepdims=True)
    first = N_GROUPS + EXPERTS_PER_GROUP * g_idx
    emask = (lane >= first) & (lane < first + EXPERTS_PER_GROUP)
    el = jnp.where(emask, logits, NEG)
    v1 = jnp.max(el, axis=1, keepdims=True)
    i1 = jnp.min(jnp.where(emask & (el == v1), lane, big), axis=1, keepdims=True)
    el2 = jnp.where(lane == i1, NEG, el)
    v2 = jnp.max(el2, axis=1, keepdims=True)
    i2 = jnp.min(jnp.where(emask & (el2 == v2) & (lane != i1), lane, big), axis=1, keepdims=True)
    t = jnp.exp(v2 - v1)
    w1 = g_val / (1.0 + t)
    w2 = g_val * t / (1.0 + t)
    gate_ref[...] = jnp.where(lane == i1, w1, 0.0) + jnp.where(lane == i2, w2, 0.0)


def _tail(mix, x2d, wo_bf, g1, b1, wr_pad, br_pad, tm, alpha):
    n = x2d.shape[0]
    row = lambda i: (i, 0)
    const = lambda i: (0, 0)
    return pl.pallas_call(
        functools.partial(_tail_kernel, alpha=alpha),
        grid=(n // tm,),
        in_specs=[pl.BlockSpec((tm, D_MODEL), row), pl.BlockSpec((tm, D_MODEL), row),
                  pl.BlockSpec((D_MODEL, D_MODEL), const),
                  pl.BlockSpec((1, D_MODEL), const), pl.BlockSpec((1, D_MODEL), const),
                  pl.BlockSpec((D_MODEL, LANES), const), pl.BlockSpec((1, LANES), const)],
        out_specs=[pl.BlockSpec((tm, D_MODEL), row), pl.BlockSpec((tm, D_MODEL), row),
                   pl.BlockSpec((tm, LANES), row)],
        out_shape=[jax.ShapeDtypeStruct((n, D_MODEL), F32), jax.ShapeDtypeStruct((n, D_MODEL), BF16),
                   jax.ShapeDtypeStruct((n, LANES), F32)],
        compiler_params=_cparams(("arbitrary",)),
        name="tail",
    )(mix, x2d, wo_bf, g1, b1, wr_pad, br_pad)


def _moe_kernel(hb_ref, h_ref, gate_ref, wg_ref, wu_ref, wd_ref, g_ref, b_ref, y_ref, acc_ref, *, alpha):
    e = pl.program_id(1)

    @pl.when(e == 0)
    def _():
        acc_ref[...] = jnp.zeros(acc_ref.shape, F32)

    xb = hb_ref[...]
    tm = xb.shape[0]
    lane = lax.broadcasted_iota(jnp.int32, (tm, LANES), 1)
    gate = jnp.sum(jnp.where(lane == e + N_GROUPS, gate_ref[...], 0.0), axis=1, keepdims=True)
    a = _dot(xb, wg_ref[0])
    u = _dot(xb, wu_ref[0])
    hmid = (a * jax.nn.sigmoid(a)) * u * gate
    acc_ref[...] += _dot(hmid.astype(BF16), wd_ref[0])

    @pl.when(e == pl.num_programs(1) - 1)
    def _():
        y_ref[...] = _layer_norm(alpha * h_ref[...] + acc_ref[...], g_ref[...], b_ref[...])


def _moe(hb, h, gates, wg_bf, wu_bf, wd_bf, g2, b2, tm, alpha):
    n = h.shape[0]
    row = lambda i, e: (i, 0)
    const = lambda i, e: (0, 0)
    return pl.pallas_call(
        functools.partial(_moe_kernel, alpha=alpha),
        grid=(n // tm, N_EXPERTS),
        in_specs=[pl.BlockSpec((tm, D_MODEL), row), pl.BlockSpec((tm, D_MODEL), row),
                  pl.BlockSpec((tm, LANES), row),
                  pl.BlockSpec((1, D_MODEL, D_EXPERT), lambda i, e: (e, 0, 0)),
                  pl.BlockSpec((1, D_MODEL, D_EXPERT), lambda i, e: (e, 0, 0)),
                  pl.BlockSpec((1, D_EXPERT, D_MODEL), lambda i, e: (e, 0, 0)),
                  pl.BlockSpec((1, D_MODEL), const), pl.BlockSpec((1, D_MODEL), const)],
        out_specs=pl.BlockSpec((tm, D_MODEL), row),
        out_shape=jax.ShapeDtypeStruct((n, D_MODEL), F32),
        scratch_shapes=[pltpu.VMEM((tm, D_MODEL), F32)],
        compiler_params=_cparams(("arbitrary", "arbitrary")),
        name="moe",
    )(hb, h, gates, wg_bf, wu_bf, wd_bf, g2, b2)


def _rope_tables(pos):
    half = HEAD_DIM // 2
    inv = ROPE_THETA ** (-jnp.arange(half, dtype=F32) * 2.0 / HEAD_DIM)
    ang = pos.astype(F32)[:, None] * inv[None, :]
    cos, sin = jnp.cos(ang), jnp.sin(ang)
    reps = WIDTH // HEAD_DIM
    cos_t = jnp.tile(jnp.concatenate([cos, cos], axis=1), (1, reps))
    sin_t = jnp.tile(jnp.concatenate([-sin, sin], axis=1), (1, reps))
    return cos_t, sin_t


def kernel(x_prompt, x_sample, cache_diff_k, cache_diff_v, cache_fox_k, cache_fox_v, cache_fox_logf, page_table,
           w_in, b_forget, lambda_q1, lambda_k1, lambda_q2, lambda_k2, subln_g, w_out, ln1_g, ln1_b,
           w_router_group, b_router_group, w_router_expert, b_router_expert, w_gate, w_up, w_down, ln2_g, ln2_b):
    B, S, D = x_prompt.shape
    DB, T, _ = x_sample.shape
    depth = w_in.shape[0]
    assert depth == 1 and T == 1 and D == D_MODEL
    n_pool, page = cache_diff_k.shape[1], cache_diff_k.shape[2]
    n_pages = page_table.shape[1]
    past_len = n_pages * page
    assert page == LANES
    alpha = (2 * depth) ** 0.25
    l = 0
    lam_init = 0.8 - 0.6 * math.exp(-0.3 * l)

    w_pad = jnp.pad(w_in[l], ((0, 0), (0, 6 * WIDTH + LANES - w_in.shape[2]))).astype(BF16)
    bf_pad = jnp.pad(b_forget[l], (0, LANES - FOX_HEADS)).reshape(1, LANES)
    lamv = jnp.stack([lambda_q1[l], lambda_k1[l], lambda_q2[l], lambda_k2[l]])
    g1 = subln_g[l].reshape(1, LANES)
    g4 = jnp.tile(g1, (1, DIFF_HEADS))
    wo_bf = w_out[l].astype(BF16)
    wr_pad = jnp.pad(jnp.concatenate([w_router_group[l], w_router_expert[l]], axis=1),
                     ((0, 0), (0, LANES - N_GROUPS - N_EXPERTS)))
    br_pad = jnp.pad(jnp.concatenate([b_router_group[l], b_router_expert[l]]),
                     (0, LANES - N_GROUPS - N_EXPERTS)).reshape(1, LANES)
    wg_bf, wu_bf, wd_bf = w_gate[l].astype(BF16), w_up[l].astype(BF16), w_down[l].astype(BF16)
    ln1g, ln1b = ln1_g[l].reshape(1, D), ln1_b[l].reshape(1, D)
    ln2g, ln2b = ln2_g[l].reshape(1, D), ln2_b[l].reshape(1, D)
    p_cos, p_sin = _rope_tables(jnp.arange(S, dtype=jnp.int32))
    s_cos, s_sin = _rope_tables(jnp.full((DB,), past_len, dtype=jnp.int32))

    tm = 256
    xp2 = x_prompt.reshape(B * S, D)
    (dq, dkb, dvb, fq, fkb, fvb, dk, dv, fk, fv, logf) = _project(xp2, w_pad, p_cos, p_sin, bf_pad, tm, S // tm)
    logf_t = jnp.swapaxes(logf.reshape(B, S, FOX_HEADS), 1, 2).reshape(B * FOX_HEADS, S)
    c_t = _cumsum_rows(logf_t).reshape(B, FOX_HEADS, S)
    tq = 256
    c = jnp.swapaxes(c_t, 1, 2)
    c_blk = jnp.swapaxes(c_t.reshape(B, FOX_HEADS, S // tq, tq), 1, 2)
    r3 = lambda a: a.reshape(B, S, WIDTH)
    mix_p = _prompt_attention(r3(dq), r3(fq), r3(dkb), r3(dvb), r3(fkb), r3(fvb), c, c_blk, lamv, g1,
                              tq, lam_init)
    h_p, hb_p, gates_p = _tail(mix_p.reshape(B * S, D), xp2, wo_bf, ln1g, ln1b, wr_pad, br_pad, 256, alpha)
    y_p = _moe(hb_p, h_p, gates_p, wg_bf, wu_bf, wd_bf, ln2g, ln2b, 1024, alpha)

    xs2 = x_sample.reshape(DB, D)
    (sdq, _, _, sfq, _, _, sdk, sdv, sfk, sfv, slogf) = _project(xs2, w_pad, s_cos, s_sin, bf_pad, DB, 1)
    lft = jnp.swapaxes(cache_fox_logf[l], 1, 2).reshape(n_pool * FOX_HEADS, page)
    ex, tot = _page_suffix(lft, 4096)
    v3 = lambda a: a.reshape(DB, 1, WIDTH)
    pool = lambda a: a[l].reshape(n_pool, page, WIDTH)
    mix_s = _decode_attention(page_table, v3(sdq.astype(F32)), v3(sfq.astype(F32)), v3(sdk), v3(sdv), v3(sfk),
                              v3(sfv), slogf.reshape(DB, FOX_HEADS, 1), lamv, g4,
                              pool(cache_diff_k), pool(cache_diff_v), pool(cache_fox_k), pool(cache_fox_v),
                              ex.reshape(n_pool, FOX_HEADS, page), tot.reshape(n_pool, FOX_HEADS, page),
                              8, lam_init)
    h_s, hb_s, gates_s = _tail(mix_s.reshape(DB, D), xs2, wo_bf, ln1g, ln1b, wr_pad, br_pad, DB, alpha)
    y_s = _moe(hb_s, h_s, gates_s, wg_bf, wu_bf, wd_bf, ln2g, ln2b, DB, alpha)

    return (y_p.reshape(B, S, D), y_s.reshape(DB, T, D),
            dk.reshape(1, B, S, DIFF_HEADS, 2 * HEAD_DIM), dv.reshape(1, B, S, DIFF_HEADS, 2 * HEAD_DIM),
            fk.reshape(1, B, S, FOX_HEADS, HEAD_DIM), fv.reshape(1, B, S, FOX_HEADS, HEAD_DIM),
            logf.reshape(1, B, S, FOX_HEADS),
            sdk.reshape(1, DB, T, DIFF_HEADS, 2 * HEAD_DIM), sdv.reshape(1, DB, T, DIFF_HEADS, 2 * HEAD_DIM),
            sfk.reshape(1, DB, T, FOX_HEADS, HEAD_DIM), sfv.reshape(1, DB, T, FOX_HEADS, HEAD_DIM),
            slogf.reshape(1, DB, T, FOX_HEADS))
```

```python
import functools
import math

import jax
import jax.numpy as jnp
from jax import lax
from jax.experimental import pallas as pl
from jax.experimental.pallas import tpu as pltpu

F32 = jnp.float32
BF16 = jnp.bfloat16

HEAD_DIM = 64
DIFF_HEADS = 4
FOX_HEADS = 8
WIDTH = 512
D_MODEL = 1024
N_GROUPS = 4
EXPERTS_PER_GROUP = 4
N_EXPERTS = 16
D_EXPERT = 512
ROPE_THETA = 10000.0
LN_EPS = 1e-5
SUBLN_EPS = 1e-5
LANES = 128
NEG = -1e30
LOG2E = 1.4426950408889634
V7X_VMEM_LIMIT = 48 * 1024 * 1024


def _cparams(sem):
    return pltpu.CompilerParams(dimension_semantics=sem, vmem_limit_bytes=V7X_VMEM_LIMIT)


def _dot(a, b):
    return jnp.dot(a, b, preferred_element_type=F32)


def _dot_nt(a, b):
    return lax.dot_general(a, b, (((1,), (1,)), ((), ())), preferred_element_type=F32)


def _dot_hi(a, b):
    return jnp.dot(a, b, preferred_element_type=F32, precision=lax.Precision.HIGHEST)


def _layer_norm(x, g, b):
    mu = jnp.mean(x, axis=-1, keepdims=True)
    xc = x - mu
    var = jnp.mean(xc * xc, axis=-1, keepdims=True)
    return xc * lax.rsqrt(var + LN_EPS) * g + b


def _diff_lambda(lv, lam_init):
    return (jnp.exp(jnp.sum(lv[0:1] * lv[1:2], axis=1, keepdims=True))
            - jnp.exp(jnp.sum(lv[2:3] * lv[3:4], axis=1, keepdims=True)) + lam_init)


def _proj_kernel(x_ref, w_ref, cos_ref, sin_ref, bf_ref,
                 dq_ref, dkb_ref, dvb_ref, fq_ref, fkb_ref, fvb_ref,
                 dk_ref, dv_ref, fk_ref, fv_ref, logf_ref):
    tm = x_ref.shape[0]
    xb = x_ref[...].astype(BF16)
    cos = cos_ref[...]
    sin = sin_ref[...]
    lane = lax.broadcasted_iota(jnp.int32, (tm, WIDTH), 1)
    first_half = (lane & (HEAD_DIM // 2)) == 0
    qscale = HEAD_DIM ** -0.5 * LOG2E

    def mm(c):
        return _dot(xb, w_ref[:, c * WIDTH:(c + 1) * WIDTH])

    def rope(p):
        partner = jnp.where(first_half,
                            pltpu.roll(p, WIDTH - HEAD_DIM // 2, 1),
                            pltpu.roll(p, HEAD_DIM // 2, 1))
        return p * cos + partner * sin

    dq = rope(mm(0))
    dq_ref[...] = (dq * qscale).astype(BF16)
    dk = rope(mm(1))
    dk_ref[...] = dk
    dkb_ref[...] = dk.astype(BF16)
    dv = mm(2)
    dv_ref[...] = dv
    dvb_ref[...] = dv.astype(BF16)
    fq_ref[...] = (mm(3) * qscale).astype(BF16)
    fk = mm(4)
    fk_ref[...] = fk
    fkb_ref[...] = fk.astype(BF16)
    fv = mm(5)
    fv_ref[...] = fv
    fvb_ref[...] = fv.astype(BF16)
    z = _dot(xb, w_ref[:, 6 * WIDTH:6 * WIDTH + LANES]) + bf_ref[...]
    logf = jnp.minimum(z, 0.0) - jnp.log1p(jnp.exp(-jnp.abs(z)))
    logf_ref[...] = logf[:, :FOX_HEADS]


def _project(x2d, w_bf, cos_t, sin_t, bf_pad, tm, table_blocks):
    n = x2d.shape[0]
    grid = (n // tm,)
    row = lambda i: (i, 0)
    tab = lambda i: (i % table_blocks, 0)
    const = lambda i: (0, 0)
    wide = pl.BlockSpec((tm, WIDTH), row)
    out_shape = ([jax.ShapeDtypeStruct((n, WIDTH), BF16)] * 6
                 + [jax.ShapeDtypeStruct((n, WIDTH), F32)] * 4
                 + [jax.ShapeDtypeStruct((n, FOX_HEADS), F32)])
    return pl.pallas_call(
        _proj_kernel,
        grid=grid,
        in_specs=[pl.BlockSpec((tm, D_MODEL), row),
                  pl.BlockSpec(w_bf.shape, const),
                  pl.BlockSpec((tm, WIDTH), tab),
                  pl.BlockSpec((tm, WIDTH), tab),
                  pl.BlockSpec((1, LANES), const)],
        out_specs=[wide] * 10 + [pl.BlockSpec((tm, FOX_HEADS), row)],
        out_shape=out_shape,
        compiler_params=_cparams(("arbitrary",)),
        name="proj",
    )(x2d, w_bf, cos_t, sin_t, bf_pad)


def _cumsum_kernel(x_ref, o_ref):
    rows, s = x_ref.shape
    t = lax.broadcasted_iota(jnp.int32, (LANES, LANES), 0)
    u = lax.broadcasted_iota(jnp.int32, (LANES, LANES), 1)
    tri = (t <= u).astype(F32)
    carry = jnp.zeros((rows, 1), F32)
    for ch in range(s // LANES):
        blk = _dot_hi(x_ref[:, ch * LANES:(ch + 1) * LANES], tri) + carry
        o_ref[:, ch * LANES:(ch + 1) * LANES] = blk * LOG2E
        carry = blk[:, LANES - 1:LANES]


def _cumsum_rows(xt):
    return pl.pallas_call(
        _cumsum_kernel,
        out_shape=jax.ShapeDtypeStruct(xt.shape, F32),
        name="cumsum",
    )(xt)


def _attn_kernel(dq_ref, fq_ref, dk_ref, dv_ref, fk_ref, fv_ref, c_ref, ct_ref, lamv_ref, g_ref,
                 mix_ref, *, tq, lam_init, group):
    qi = pl.program_id(1)
    tk = tq
    lane = lax.broadcasted_iota(jnp.int32, (tq, LANES), 1)
    lo = lane < HEAD_DIM
    rowi = lax.broadcasted_iota(jnp.int32, (tq, tk), 0)
    coli = lax.broadcasted_iota(jnp.int32, (tq, tk), 1)
    causal = coli <= rowi
    lam = _diff_lambda(lamv_ref[...], lam_init)
    g = g_ref[...]
    n_units = DIFF_HEADS + FOX_HEADS // 2

    for g0 in range(0, n_units, group):
        units = []
        for u in range(g0, g0 + group):
            is_diff = u < DIFF_HEADS
            off = (u if is_diff else u - DIFF_HEADS) * LANES
            q_pair = (dq_ref if is_diff else fq_ref)[0, :, off:off + LANES]
            zero = jnp.zeros_like(q_pair)
            h0 = None if is_diff else 2 * (u - DIFF_HEADS)
            cq = None if is_diff else (c_ref[0, :, h0:h0 + 1], c_ref[0, :, h0 + 1:h0 + 2])
            units.append(dict(q=(jnp.where(lo, q_pair, zero), jnp.where(lo, zero, q_pair)),
                              k_ref=dk_ref if is_diff else fk_ref, v_ref=dv_ref if is_diff else fv_ref,
                              off=off, is_diff=is_diff, h0=h0, cq=cq))

        def step(j, carry, masked, units=units):
            out = []
            start = pl.multiple_of(j * tk, tk)
            for ui, un in enumerate(units):
                off = un["off"]
                k = un["k_ref"][0, pl.ds(start, tk), off:off + LANES]
                v = un["v_ref"][0, pl.ds(start, tk), off:off + LANES]
                for idx in range(2):
                    m, l, acc = carry[2 * ui + idx]
                    s = _dot_nt(un["q"][idx], k)
                    if not un["is_diff"]:
                        h = un["h0"] + idx
                        s = s + (un["cq"][idx] - ct_ref[0, j, h:h + 1, :])
                    if masked:
                        s = jnp.where(causal, s, NEG)
                    m_new = jnp.maximum(m, jnp.max(s, axis=1, keepdims=True))
                    alpha = jnp.exp2(m - m_new)
                    e = jnp.exp2(s - m_new)
                    l = alpha * l + jnp.sum(e, axis=1, keepdims=True)
                    acc = alpha * acc + _dot(e.astype(BF16), v)
                    out.append((m_new, l, acc))
            return tuple(out)

        init = tuple((jnp.full((tq, 1), NEG, F32), jnp.zeros((tq, 1), F32), jnp.zeros((tq, LANES), F32))
                     for _ in range(2 * group))
        carry = lax.fori_loop(0, qi, lambda j, c: step(j, c, False), init)
        fin = step(qi, carry, True)
        for ui, un in enumerate(units):
            (_, la, acca), (_, lb, accb) = fin[2 * ui], fin[2 * ui + 1]
            off = un["off"]
            if un["is_diff"]:
                o = acca / la - lam * (accb / lb)
                ms = jnp.mean(o * o, axis=1, keepdims=True)
                o = o * lax.rsqrt(ms + SUBLN_EPS) * g * (1.0 - lam_init)
                mix_ref[0, :, off:off + LANES] = o.astype(BF16)
            else:
                mix_ref[0, :, WIDTH + off:WIDTH + off + LANES] = jnp.where(lo, acca / la, accb / lb).astype(BF16)


def _prompt_attention(dq, fq, dk, dv, fk, fv, c, ct, lamv, g, tq, lam_init, group):
    b, s, _ = dq.shape
    qspec = pl.BlockSpec((1, tq, WIDTH), lambda bi, qi: (bi, qi, 0))
    kspec = pl.BlockSpec((1, s, WIDTH), lambda bi, qi: (bi, 0, 0))
    return pl.pallas_call(
        functools.partial(_attn_kernel, tq=tq, lam_init=lam_init, group=group),
        grid=(b, s // tq),
        in_specs=[qspec, qspec, kspec, kspec, kspec, kspec,
                  pl.BlockSpec((1, tq, FOX_HEADS), lambda bi, qi: (bi, qi, 0)),
                  pl.BlockSpec((1, s // tq, FOX_HEADS, tq), lambda bi, qi: (bi, 0, 0, 0)),
                  pl.BlockSpec((4, HEAD_DIM), lambda bi, qi: (0, 0)),
                  pl.BlockSpec((1, LANES), lambda bi, qi: (0, 0))],
        out_specs=pl.BlockSpec((1, tq, D_MODEL), lambda bi, qi: (bi, qi, 0)),
        out_shape=jax.ShapeDtypeStruct((b, s, D_MODEL), BF16),
        compiler_params=_cparams(("arbitrary", "arbitrary")),
        name="attn",
    )(dq, fq, dk, dv, fk, fv, c, ct, lamv, g)


def _suffix_kernel(x_ref, e_ref, t_ref):
    t = lax.broadcasted_iota(jnp.int32, (LANES, LANES), 0)
    u = lax.broadcasted_iota(jnp.int32, (LANES, LANES), 1)
    x = x_ref[...]
    e_ref[...] = _dot_hi(x, (t > u).astype(F32))
    t_ref[...] = _dot_hi(x, jnp.ones((LANES, LANES), F32))


def _page_suffix(lft, tr):
    n = lft.shape[0]
    spec = pl.BlockSpec((tr, LANES), lambda i: (i, 0))
    return pl.pallas_call(
        _suffix_kernel,
        grid=(n // tr,),
        in_specs=[spec],
        out_specs=[spec, spec],
        out_shape=[jax.ShapeDtypeStruct(lft.shape, F32)] * 2,
        compiler_params=_cparams(("arbitrary",)),
        name="suffix",
    )(lft)


def _decode_kernel(pt_ref, qd_ref, qf_ref, knd_ref, vnd_ref, knf_ref, vnf_ref, slogf_ref, lamv_ref, g_ref,
                   *rest, pages, lam_init):
    del pt_ref
    kd = rest[0:pages]
    vd = rest[pages:2 * pages]
    kf = rest[2 * pages:3 * pages]
    vf = rest[3 * pages:4 * pages]
    ex = rest[4 * pages:5 * pages]
    tot = rest[5 * pages:6 * pages]
    od_ref, of_ref = rest[6 * pages], rest[6 * pages + 1]
    md, ld, accd, mf, lf, accf, carry = rest[6 * pages + 2:]
    j = pl.program_id(1)
    rows = 2 * DIFF_HEADS
    th = DIFF_HEADS * LANES

    @pl.when(j == 0)
    def _():
        md[...] = jnp.full(md.shape, NEG, F32)
        mf[...] = jnp.full(mf.shape, NEG, F32)
        ld[...] = jnp.zeros(ld.shape, F32)
        lf[...] = jnp.zeros(lf.shape, F32)
        accd[...] = jnp.zeros(accd.shape, F32)
        accf[...] = jnp.zeros(accf.shape, F32)
        carry[...] = jnp.zeros(carry.shape, F32)

    lane1 = lax.broadcasted_iota(jnp.int32, (rows, LANES), 1)
    r1 = lax.broadcasted_iota(jnp.int32, (rows, LANES), 0)
    qd = jnp.where((lane1 >> 6) == (r1 & 1), qd_ref[0], 0.0)
    col = lax.broadcasted_iota(jnp.int32, (rows, th), 1)
    r = lax.broadcasted_iota(jnp.int32, (rows, th), 0)
    own_head = (col & (DIFF_HEADS - 1)) == (r >> 1)
    half_mask = (col >> 6) == r
    qf = jnp.where(half_mask, qf_ref[0], 0.0)

    def update(m_ref, l_ref, acc_ref, s_list, pv_fn):
        s = jnp.concatenate(s_list, axis=1)
        w = s.shape[1] // pages
        m_old = m_ref[...]
        m_new = jnp.maximum(m_old, jnp.max(s, axis=1, keepdims=True))
        alpha = jnp.exp2(m_old - m_new)
        e = jnp.exp2(s - m_new)
        l_ref[...] = alpha * l_ref[...] + jnp.sum(e, axis=1, keepdims=True)
        pv = pv_fn(e[:, 0:w], 0)
        for k in range(1, pages):
            pv = pv + pv_fn(e[:, k * w:(k + 1) * w], k)
        acc_ref[...] = alpha * acc_ref[...] + pv
        m_ref[...] = m_new

    sd = [jnp.where(own_head, _dot_nt(qd, kd[k][...]), NEG) for k in range(pages)]
    update(md, ld, accd, sd, lambda e, k: _dot(e, vd[k][...]))

    run = carry[...]
    base = slogf_ref[0]
    sf = [None] * pages
    for k in reversed(range(pages)):
        sf[k] = _dot(qf, kf[k][...]) + LOG2E * ((base + run) + ex[k][...])
        run = run + tot[k][...]
    carry[...] = run
    update(mf, lf, accf, sf, lambda e, k: _dot_nt(e, vf[k][...]))

    @pl.when(j == pl.num_programs(1) - 1)
    def _():
        lam = _diff_lambda(lamv_ref[...], lam_init)

        def finish(m_ref, l_ref, acc_ref, q8, k_new, v_new):
            s_new = jnp.sum(q8 * k_new, axis=1, keepdims=True)
            m_old = m_ref[...]
            m_fin = jnp.maximum(m_old, s_new)
            a = jnp.exp2(m_old - m_fin)
            en = jnp.exp2(s_new - m_fin)
            l_fin = a * l_ref[...] + en
            return (a * acc_ref[...] + en * v_new) / l_fin

        od8 = finish(md, ld, accd, qd, knd_ref[0], vnd_ref[0])
        comb = od8 * jnp.where((r1 & 1) == 0, 1.0, -lam)
        o = comb + pltpu.roll(comb, rows - 1, 0)
        ms = jnp.mean(o * o, axis=1, keepdims=True)
        od_ref[0] = o * lax.rsqrt(ms + SUBLN_EPS) * g_ref[...] * (1.0 - lam_init)
        of8 = finish(mf, lf, accf, qf, knf_ref[0], vnf_ref[0])
        of_ref[0] = jnp.sum(jnp.where(half_mask, of8, 0.0), axis=0, keepdims=True)


def _decode_attention(page_table, qd, qf, knd, vnd, knf, vnf, slogf, lamv, g1,
                      ckd, cvd, ckf, cvf, ex, tot, pages, lam_init):
    db, n_pages = page_table.shape
    n_steps = n_pages // pages
    rows = FOX_HEADS

    def page_map(k):
        return lambda b, j, pt: (pt[b, (n_steps - 1 - j) * pages + k], 0, 0)

    per_seq = lambda b, j, pt: (b, 0, 0)
    const = lambda b, j, pt: (0, 0)
    r128 = pl.BlockSpec((1, rows, LANES), per_seq)
    v512 = pl.BlockSpec((1, 1, WIDTH), per_seq)
    in_specs = [r128, v512, r128, r128, v512, v512,
                pl.BlockSpec((1, FOX_HEADS, 1), per_seq),
                pl.BlockSpec((4, HEAD_DIM), const),
                pl.BlockSpec((1, LANES), const)]
    for _ in range(4):
        in_specs += [pl.BlockSpec((None, WIDTH, LANES), page_map(k)) for k in range(pages)]
    for _ in range(2):
        in_specs += [pl.BlockSpec((None, FOX_HEADS, LANES), page_map(k)) for k in range(pages)]
    grid_spec = pltpu.PrefetchScalarGridSpec(
        num_scalar_prefetch=1,
        grid=(db, n_steps),
        in_specs=in_specs,
        out_specs=[r128, v512],
        scratch_shapes=[pltpu.VMEM((rows, 1), F32), pltpu.VMEM((rows, 1), F32), pltpu.VMEM((rows, LANES), F32),
                        pltpu.VMEM((rows, 1), F32), pltpu.VMEM((rows, 1), F32), pltpu.VMEM((rows, WIDTH), F32),
                        pltpu.VMEM((rows, LANES), F32)],
    )
    args = ([qd, qf, knd, vnd, knf, vnf, slogf, lamv, g1]
            + [ckd] * pages + [cvd] * pages + [ckf] * pages + [cvf] * pages + [ex] * pages + [tot] * pages)
    return pl.pallas_call(
        functools.partial(_decode_kernel, pages=pages, lam_init=lam_init),
        grid_spec=grid_spec,
        out_shape=[jax.ShapeDtypeStruct((db, rows, LANES), F32), jax.ShapeDtypeStruct((db, 1, WIDTH), F32)],
        compiler_params=_cparams(("arbitrary", "arbitrary")),
        name="decode",
    )(page_table, *args)


def _tail_kernel(mix_ref, x_ref, wo_ref, g_ref, b_ref, wr_ref, br_ref, h_ref, hb_ref, gate_ref, *, alpha):
    y = _dot(mix_ref[...].astype(BF16), wo_ref[...])
    h = _layer_norm(alpha * x_ref[...] + y, g_ref[...], b_ref[...])
    h_ref[...] = h
    hb_ref[...] = h.astype(BF16)
    logits = _dot_hi(h, wr_ref[...]) + br_ref[...]
    tm = logits.shape[0]
    lane = lax.broadcasted_iota(jnp.int32, (tm, LANES), 1).astype(F32)
    big = float(LANES)
    gmask = lane < N_GROUPS
    gl = jnp.where(gmask, logits, NEG)
    gmax = jnp.max(gl, axis=1, keepdims=True)
    gsum = jnp.sum(jnp.where(gmask, jnp.exp(gl - gmax), 0.0), axis=1, keepdims=True)
    g_val = 1.0 / gsum
    g_idx = jnp.min(jnp.where(gmask & (gl == gmax), lane, big), axis=1, keepdims=True)
    first = N_GROUPS + EXPERTS_PER_GROUP * g_idx
    emask = (lane >= first) & (lane < first + EXPERTS_PER_GROUP)
    el = jnp.where(emask, logits, NEG)
    v1 = jnp.max(el, axis=1, keepdims=True)
    i1 = jnp.min(jnp.where(emask & (el == v1), lane, big), axis=1, keepdims=True)
    el2 = jnp.where(lane == i1, NEG, el)
    v2 = jnp.max(el2, axis=1, keepdims=True)
    i2 = jnp.min(jnp.where(emask & (el2 == v2) & (lane != i1), lane, big), axis=1, keepdims=True)
    t = jnp.exp(v2 - v1)
    w1 = g_val / (1.0 + t)
    w2 = g_val * t / (1.0 + t)
    gate_ref[...] = jnp.where(lane == i1, w1, 0.0) + jnp.where(lane == i2, w2, 0.0)


def _tail(mix, x2d, wo_bf, g1, b1, wr_pad, br_pad, tm, alpha):
    n = x2d.shape[0]
    row = lambda i: (i, 0)
    const = lambda i: (0, 0)
    return pl.pallas_call(
        functools.partial(_tail_kernel, alpha=alpha),
        grid=(n // tm,),
        in_specs=[pl.BlockSpec((tm, D_MODEL), row), pl.BlockSpec((tm, D_MODEL), row),
                  pl.BlockSpec((D_MODEL, D_MODEL), const),
                  pl.BlockSpec((1, D_MODEL), const), pl.BlockSpec((1, D_MODEL), const),
                  pl.BlockSpec((D_MODEL, LANES), const), pl.BlockSpec((1, LANES), const)],
        out_specs=[pl.BlockSpec((tm, D_MODEL), row), pl.BlockSpec((tm, D_MODEL), row),
                   pl.BlockSpec((tm, LANES), row)],
        out_shape=[jax.ShapeDtypeStruct((n, D_MODEL), F32), jax.ShapeDtypeStruct((n, D_MODEL), BF16),
                   jax.ShapeDtypeStruct((n, LANES), F32)],
        compiler_params=_cparams(("arbitrary",)),
        name="tail",
    )(mix, x2d, wo_bf, g1, b1, wr_pad, br_pad)


def _moe_kernel(hb_ref, h_ref, gate_ref, wg_ref, wu_ref, wd_ref, g_ref, b_ref, y_ref, acc_ref, *, alpha):
    e = pl.program_id(1)

    @pl.when(e == 0)
    def _():
        acc_ref[...] = jnp.zeros(acc_ref.shape, F32)

    xb = hb_ref[...]
    tm = xb.shape[0]
    lane = lax.broadcasted_iota(jnp.int32, (tm, LANES), 1)
    gate = jnp.sum(jnp.where(lane == e + N_GROUPS, gate_ref[...], 0.0), axis=1, keepdims=True)
    a = _dot(xb, wg_ref[0])
    u = _dot(xb, wu_ref[0])
    hmid = (a * jax.nn.sigmoid(a)) * u * gate
    acc_ref[...] += _dot(hmid.astype(BF16), wd_ref[0])

    @pl.when(e == pl.num_programs(1) - 1)
    def _():
        y_ref[...] = _layer_norm(alpha * h_ref[...] + acc_ref[...], g_ref[...], b_ref[...])


def _moe(hb, h, gates, wg_bf, wu_bf, wd_bf, g2, b2, tm, alpha):
    n = h.shape[0]
    row = lambda i, e: (i, 0)
    const = lambda i, e: (0, 0)
    return pl.pallas_call(
        functools.partial(_moe_kernel, alpha=alpha),
        grid=(n // tm, N_EXPERTS),
        in_specs=[pl.BlockSpec((tm, D_MODEL), row), pl.BlockSpec((tm, D_MODEL), row),
                  pl.BlockSpec((tm, LANES), row),
                  pl.BlockSpec((1, D_MODEL, D_EXPERT), lambda i, e: (e, 0, 0)),
                  pl.BlockSpec((1, D_MODEL, D_EXPERT), lambda i, e: (e, 0, 0)),
                  pl.BlockSpec((1, D_EXPERT, D_MODEL), lambda i, e: (e, 0, 0)),
                  pl.BlockSpec((1, D_MODEL), const), pl.BlockSpec((1, D_MODEL), const)],
        out_specs=pl.BlockSpec((tm, D_MODEL), row),
        out_shape=jax.ShapeDtypeStruct((n, D_MODEL), F32),
        scratch_shapes=[pltpu.VMEM((tm, D_MODEL), F32)],
        compiler_params=_cparams(("arbitrary", "arbitrary")),
        name="moe",
    )(hb, h, gates, wg_bf, wu_bf, wd_bf, g2, b2)


def _rope_tables(pos):
    half = HEAD_DIM // 2
    inv = ROPE_THETA ** (-jnp.arange(half, dtype=F32) * 2.0 / HEAD_DIM)
    ang = pos.astype(F32)[:, None] * inv[None, :]
    cos, sin = jnp.cos(ang), jnp.sin(ang)
    reps = WIDTH // HEAD_DIM
    cos_t = jnp.tile(jnp.concatenate([cos, cos], axis=1), (1, reps))
    sin_t = jnp.tile(jnp.concatenate([-sin, sin], axis=1), (1, reps))
    return cos_t, sin_t


def kernel(x_prompt, x_sample, cache_diff_k, cache_diff_v, cache_fox_k, cache_fox_v, cache_fox_logf, page_table,
           w_in, b_forget, lambda_q1, lambda_k1, lambda_q2, lambda_k2, subln_g, w_out, ln1_g, ln1_b,
           w_router_group, b_router_group, w_router_expert, b_router_expert, w_gate, w_up, w_down, ln2_g, ln2_b):
    B, S, D = x_prompt.shape
    DB, T, _ = x_sample.shape
    depth = w_in.shape[0]
    assert depth == 1 and T == 1 and D == D_MODEL
    n_pool, page = cache_diff_k.shape[1], cache_diff_k.shape[2]
    n_pages = page_table.shape[1]
    past_len = n_pages * page
    assert page == LANES
    alpha = (2 * depth) ** 0.25
    l = 0
    lam_init = 0.8 - 0.6 * math.exp(-0.3 * l)

    w_pad = jnp.pad(w_in[l], ((0, 0), (0, 6 * WIDTH + LANES - w_in.shape[2]))).astype(BF16)
    bf_pad = jnp.pad(b_forget[l], (0, LANES - FOX_HEADS)).reshape(1, LANES)
    lamv = jnp.stack([lambda_q1[l], lambda_k1[l], lambda_q2[l], lambda_k2[l]])
    g1 = subln_g[l].reshape(1, LANES)
    wo_bf = w_out[l].astype(BF16)
    wr_pad = jnp.pad(jnp.concatenate([w_router_group[l], w_router_expert[l]], axis=1),
                     ((0, 0), (0, LANES - N_GROUPS - N_EXPERTS)))
    br_pad = jnp.pad(jnp.concatenate([b_router_group[l], b_router_expert[l]]),
                     (0, LANES - N_GROUPS - N_EXPERTS)).reshape(1, LANES)
    wg_bf, wu_bf, wd_bf = w_gate[l].astype(BF16), w_up[l].astype(BF16), w_down[l].astype(BF16)
    ln1g, ln1b = ln1_g[l].reshape(1, D), ln1_b[l].reshape(1, D)
    ln2g, ln2b = ln2_g[l].reshape(1, D), ln2_b[l].reshape(1, D)
    p_cos, p_sin = _rope_tables(jnp.arange(S, dtype=jnp.int32))
    s_cos, s_sin = _rope_tables(jnp.full((DB,), past_len, dtype=jnp.int32))

    tm = 256
    xp2 = x_prompt.reshape(B * S, D)
    (dq, dkb, dvb, fq, fkb, fvb, dk, dv, fk, fv, logf) = _project(xp2, w_pad, p_cos, p_sin, bf_pad, tm, S // tm)
    logf_t = jnp.swapaxes(logf.reshape(B, S, FOX_HEADS), 1, 2).reshape(B * FOX_HEADS, S)
    c_t = _cumsum_rows(logf_t).reshape(B, FOX_HEADS, S)
    tq = 512
    c = jnp.swapaxes(c_t, 1, 2)
    c_blk = jnp.swapaxes(c_t.reshape(B, FOX_HEADS, S // tq, tq), 1, 2)
    r3 = lambda a: a.reshape(B, S, WIDTH)
    mix_p = _prompt_attention(r3(dq), r3(fq), r3(dkb), r3(dvb), r3(fkb), r3(fvb), c, c_blk, lamv, g1,
                              tq, lam_init, 1)
    h_p, hb_p, gates_p = _tail(mix_p.reshape(B * S, D), xp2, wo_bf, ln1g, ln1b, wr_pad, br_pad, 256, alpha)
    y_p = _moe(hb_p, h_p, gates_p, wg_bf, wu_bf, wd_bf, ln2g, ln2b, 1024, alpha)

    xs2 = x_sample.reshape(DB, D)
    (sdq, _, _, sfq, _, _, sdk, sdv, sfk, sfv, slogf) = _project(xs2, w_pad, s_cos, s_sin, bf_pad, DB, 1)
    diff_pool = lambda a: a[l].reshape(n_pool, page * DIFF_HEADS, 2 * HEAD_DIM)
    fox_pool = lambda a: jnp.transpose(a[l], (0, 2, 3, 1)).reshape(n_pool, FOX_HEADS * HEAD_DIM, page)
    lft = jnp.transpose(cache_fox_logf[l], (0, 2, 1)).reshape(n_pool * FOX_HEADS, page)
    ex, tot = _page_suffix(lft, 4096)
    rep = lambda a: jnp.repeat(a.astype(F32).reshape(DB, DIFF_HEADS, 2 * HEAD_DIM), 2, axis=1)
    v3 = lambda a: a.astype(F32).reshape(DB, 1, WIDTH)
    od_s, of_s = _decode_attention(page_table, rep(sdq), v3(sfq), rep(sdk), rep(sdv), v3(sfk), v3(sfv),
                                   slogf.reshape(DB, FOX_HEADS, 1), lamv, g1,
                                   diff_pool(cache_diff_k), diff_pool(cache_diff_v),
                                   fox_pool(cache_fox_k), fox_pool(cache_fox_v),
                                   ex.reshape(n_pool, FOX_HEADS, page), tot.reshape(n_pool, FOX_HEADS, page),
                                   16, lam_init)
    mix_s = jnp.concatenate([od_s[:, 0::2, :].reshape(DB, WIDTH), of_s.reshape(DB, WIDTH)], axis=1)
    h_s, hb_s, gates_s = _tail(mix_s, xs2, wo_bf, ln1g, ln1b, wr_pad, br_pad, DB, alpha)
    y_s = _moe(hb_s, h_s, gates_s, wg_bf, wu_bf, wd_bf, ln2g, ln2b, DB, alpha)

    return (y_p.reshape(B, S, D), y_s.reshape(DB, T, D),
            dk.reshape(1, B, S, DIFF_HEADS, 2 * HEAD_DIM), dv.reshape(1, B, S, DIFF_HEADS, 2 * HEAD_DIM),
            fk.reshape(1, B, S, FOX_HEADS, HEAD_DIM), fv.reshape(1, B, S, FOX_HEADS, HEAD_DIM),
            logf.reshape(1, B, S, FOX_HEADS),
            sdk.reshape(1, DB, T, DIFF_HEADS, 2 * HEAD_DIM), sdv.reshape(1, DB, T, DIFF_HEADS, 2 * HEAD_DIM),
            sfk.reshape(1, DB, T, FOX_HEADS, HEAD_DIM), sfv.reshape(1, DB, T, FOX_HEADS, HEAD_DIM),
            slogf.reshape(1, DB, T, FOX_HEADS))
```

```python
import functools
import math

import jax
import jax.numpy as jnp
from jax import lax
from jax.experimental import pallas as pl
from jax.experimental.pallas import tpu as pltpu

F32 = jnp.float32
BF16 = jnp.bfloat16

HEAD_DIM = 64
DIFF_HEADS = 4
FOX_HEADS = 8
WIDTH = 512
D_MODEL = 1024
N_GROUPS = 4
EXPERTS_PER_GROUP = 4
N_EXPERTS = 16
D_EXPERT = 512
ROPE_THETA = 10000.0
LN_EPS = 1e-5
SUBLN_EPS = 1e-5
LANES = 128
NEG = -1e30
LOG2E = 1.4426950408889634
V7X_VMEM_LIMIT = 48 * 1024 * 1024
DECODE_PAGES = 16
MOE_TILE = 1024
MOE_GROUP_CAP = 320


def _cparams(sem):
    return pltpu.CompilerParams(dimension_semantics=sem, vmem_limit_bytes=V7X_VMEM_LIMIT)


def _dot(a, b):
    return jnp.dot(a, b, preferred_element_type=F32)


def _dot_nt(a, b):
    return lax.dot_general(a, b, (((1,), (1,)), ((), ())), preferred_element_type=F32)


def _dot_hi(a, b):
    return jnp.dot(a, b, preferred_element_type=F32, precision=lax.Precision.HIGHEST)


def _layer_norm(x, g, b):
    mu = jnp.mean(x, axis=-1, keepdims=True)
    xc = x - mu
    var = jnp.mean(xc * xc, axis=-1, keepdims=True)
    return xc * lax.rsqrt(var + LN_EPS) * g + b


def _diff_lambda(lv, lam_init):
    return (jnp.exp(jnp.sum(lv[0:1] * lv[1:2], axis=1, keepdims=True))
            - jnp.exp(jnp.sum(lv[2:3] * lv[3:4], axis=1, keepdims=True)) + lam_init)


def _proj_kernel(x_ref, w_ref, cos_ref, sin_ref, bf_ref,
                 dq_ref, dkb_ref, dvb_ref, fq_ref, fkb_ref, fvb_ref,
                 dk_ref, dv_ref, fk_ref, fv_ref, logf_ref):
    tm = x_ref.shape[0]
    xb = x_ref[...].astype(BF16)
    cos = cos_ref[...]
    sin = sin_ref[...]
    lane = lax.broadcasted_iota(jnp.int32, (tm, WIDTH), 1)
    first_half = (lane & (HEAD_DIM // 2)) == 0
    qscale = HEAD_DIM ** -0.5 * LOG2E

    def mm(c):
        return _dot(xb, w_ref[:, c * WIDTH:(c + 1) * WIDTH])

    def rope(p):
        partner = jnp.where(first_half,
                            pltpu.roll(p, WIDTH - HEAD_DIM // 2, 1),
                            pltpu.roll(p, HEAD_DIM // 2, 1))
        return p * cos + partner * sin

    dq = rope(mm(0))
    dq_ref[...] = (dq * qscale).astype(BF16)
    dk = rope(mm(1))
    dk_ref[...] = dk
    dkb_ref[...] = dk.astype(BF16)
    dv = mm(2)
    dv_ref[...] = dv
    dvb_ref[...] = dv.astype(BF16)
    fq_ref[...] = (mm(3) * qscale).astype(BF16)
    fk = mm(4)
    fk_ref[...] = fk
    fkb_ref[...] = fk.astype(BF16)
    fv = mm(5)
    fv_ref[...] = fv
    fvb_ref[...] = fv.astype(BF16)
    z = _dot(xb, w_ref[:, 6 * WIDTH:6 * WIDTH + LANES]) + bf_ref[...]
    logf = jnp.minimum(z, 0.0) - jnp.log1p(jnp.exp(-jnp.abs(z)))
    logf_ref[...] = logf[:, :FOX_HEADS]


def _project(x2d, w_bf, cos_t, sin_t, bf_pad, tm, table_blocks):
    n = x2d.shape[0]
    grid = (n // tm,)
    row = lambda i: (i, 0)
    tab = lambda i: (i % table_blocks, 0)
    const = lambda i: (0, 0)
    wide = pl.BlockSpec((tm, WIDTH), row)
    out_shape = ([jax.ShapeDtypeStruct((n, WIDTH), BF16)] * 6
                 + [jax.ShapeDtypeStruct((n, WIDTH), F32)] * 4
                 + [jax.ShapeDtypeStruct((n, FOX_HEADS), F32)])
    return pl.pallas_call(
        _proj_kernel,
        grid=grid,
        in_specs=[pl.BlockSpec((tm, D_MODEL), row),
                  pl.BlockSpec(w_bf.shape, const),
                  pl.BlockSpec((tm, WIDTH), tab),
                  pl.BlockSpec((tm, WIDTH), tab),
                  pl.BlockSpec((1, LANES), const)],
        out_specs=[wide] * 10 + [pl.BlockSpec((tm, FOX_HEADS), row)],
        out_shape=out_shape,
        compiler_params=_cparams(("arbitrary",)),
        name="proj",
    )(x2d, w_bf, cos_t, sin_t, bf_pad)


def _cumsum_kernel(x_ref, o_ref):
    rows, s = x_ref.shape
    t = lax.broadcasted_iota(jnp.int32, (LANES, LANES), 0)
    u = lax.broadcasted_iota(jnp.int32, (LANES, LANES), 1)
    tri = (t <= u).astype(F32)
    carry = jnp.zeros((rows, 1), F32)
    for ch in range(s // LANES):
        blk = _dot_hi(x_ref[:, ch * LANES:(ch + 1) * LANES], tri) + carry
        o_ref[:, ch * LANES:(ch + 1) * LANES] = blk * LOG2E
        carry = blk[:, LANES - 1:LANES]


def _cumsum_rows(xt):
    return pl.pallas_call(
        _cumsum_kernel,
        out_shape=jax.ShapeDtypeStruct(xt.shape, F32),
        name="cumsum",
    )(xt)


def _attn_kernel(dq_ref, fq_ref, dk_ref, dv_ref, fk_ref, fv_ref, c_ref, ct_ref, lamv_ref, g_ref,
                 mix_ref, *, tq, lam_init, group):
    qi = pl.program_id(1)
    tk = tq
    lane = lax.broadcasted_iota(jnp.int32, (tq, LANES), 1)
    lo = lane < HEAD_DIM
    rowi = lax.broadcasted_iota(jnp.int32, (tq, tk), 0)
    coli = lax.broadcasted_iota(jnp.int32, (tq, tk), 1)
    causal = coli <= rowi
    lam = _diff_lambda(lamv_ref[...], lam_init)
    g = g_ref[...]
    n_units = DIFF_HEADS + FOX_HEADS // 2

    for g0 in range(0, n_units, group):
        units = []
        for u in range(g0, g0 + group):
            is_diff = u < DIFF_HEADS
            off = (u if is_diff else u - DIFF_HEADS) * LANES
            q_pair = (dq_ref if is_diff else fq_ref)[0, :, off:off + LANES]
            zero = jnp.zeros_like(q_pair)
            h0 = None if is_diff else 2 * (u - DIFF_HEADS)
            cq = None if is_diff else (c_ref[0, :, h0:h0 + 1], c_ref[0, :, h0 + 1:h0 + 2])
            units.append(dict(q=(jnp.where(lo, q_pair, zero), jnp.where(lo, zero, q_pair)),
                              k_ref=dk_ref if is_diff else fk_ref, v_ref=dv_ref if is_diff else fv_ref,
                              off=off, is_diff=is_diff, h0=h0, cq=cq))

        def step(j, carry, masked, units=units):
            out = []
            start = pl.multiple_of(j * tk, tk)
            for ui, un in enumerate(units):
                off = un["off"]
                k = un["k_ref"][0, pl.ds(start, tk), off:off + LANES]
                v = un["v_ref"][0, pl.ds(start, tk), off:off + LANES]
                for idx in range(2):
                    m, l, acc = carry[2 * ui + idx]
                    s = _dot_nt(un["q"][idx], k)
                    if not un["is_diff"]:
                        h = un["h0"] + idx
                        s = s + (un["cq"][idx] - ct_ref[0, j, h:h + 1, :])
                    if masked:
                        s = jnp.where(causal, s, NEG)
                    m_new = jnp.maximum(m, jnp.max(s, axis=1, keepdims=True))
                    alpha = jnp.exp2(m - m_new)
                    e = jnp.exp2(s - m_new)
                    l = alpha * l + jnp.sum(e, axis=1, keepdims=True)
                    acc = alpha * acc + _dot(e.astype(BF16), v)
                    out.append((m_new, l, acc))
            return tuple(out)

        init = tuple((jnp.full((tq, 1), NEG, F32), jnp.zeros((tq, 1), F32), jnp.zeros((tq, LANES), F32))
                     for _ in range(2 * group))
        carry = lax.fori_loop(0, qi, lambda j, c: step(j, c, False), init)
        fin = step(qi, carry, True)
        for ui, un in enumerate(units):
            (_, la, acca), (_, lb, accb) = fin[2 * ui], fin[2 * ui + 1]
            off = un["off"]
            if un["is_diff"]:
                o = acca / la - lam * (accb / lb)
                ms = jnp.mean(o * o, axis=1, keepdims=True)
                o = o * lax.rsqrt(ms + SUBLN_EPS) * g * (1.0 - lam_init)
                mix_ref[0, :, off:off + LANES] = o.astype(BF16)
            else:
                mix_ref[0, :, WIDTH + off:WIDTH + off + LANES] = jnp.where(lo, acca / la, accb / lb).astype(BF16)


def _prompt_attention(dq, fq, dk, dv, fk, fv, c, ct, lamv, g, tq, lam_init, group):
    b, s, _ = dq.shape
    qspec = pl.BlockSpec((1, tq, WIDTH), lambda bi, qi: (bi, qi, 0))
    kspec = pl.BlockSpec((1, s, WIDTH), lambda bi, qi: (bi, 0, 0))
    return pl.pallas_call(
        functools.partial(_attn_kernel, tq=tq, lam_init=lam_init, group=group),
        grid=(b, s // tq),
        in_specs=[qspec, qspec, kspec, kspec, kspec, kspec,
                  pl.BlockSpec((1, tq, FOX_HEADS), lambda bi, qi: (bi, qi, 0)),
                  pl.BlockSpec((1, s // tq, FOX_HEADS, tq), lambda bi, qi: (bi, 0, 0, 0)),
                  pl.BlockSpec((4, HEAD_DIM), lambda bi, qi: (0, 0)),
                  pl.BlockSpec((1, LANES), lambda bi, qi: (0, 0))],
        out_specs=pl.BlockSpec((1, tq, D_MODEL), lambda bi, qi: (bi, qi, 0)),
        out_shape=jax.ShapeDtypeStruct((b, s, D_MODEL), BF16),
        compiler_params=_cparams(("arbitrary", "arbitrary")),
        name="attn",
    )(dq, fq, dk, dv, fk, fv, c, ct, lamv, g)


def _suffix_kernel(x_ref, e_ref, t_ref):
    t = lax.broadcasted_iota(jnp.int32, (LANES, LANES), 0)
    u = lax.broadcasted_iota(jnp.int32, (LANES, LANES), 1)
    x = x_ref[...]
    e_ref[...] = _dot_hi(x, (t > u).astype(F32))
    t_ref[...] = _dot_hi(x, jnp.ones((LANES, LANES), F32))


def _page_suffix(lft, tr):
    n = lft.shape[0]
    spec = pl.BlockSpec((tr, LANES), lambda i: (i, 0))
    return pl.pallas_call(
        _suffix_kernel,
        grid=(n // tr,),
        in_specs=[spec],
        out_specs=[spec, spec],
        out_shape=[jax.ShapeDtypeStruct(lft.shape, F32)] * 2,
        compiler_params=_cparams(("arbitrary",)),
        name="suffix",
    )(lft)


N_DECODE_SMALL = 9


def _decode_parts(j, n_steps, refs, pages, lam_init):
    qd_ref, qf_ref, knd_ref, vnd_ref, knf_ref, vnf_ref, slogf_ref, lamv_ref, g_ref = refs[:N_DECODE_SMALL]
    rest = refs[N_DECODE_SMALL:]
    kd = rest[0:pages]
    vd = rest[pages:2 * pages]
    kf = rest[2 * pages:3 * pages]
    vf = rest[3 * pages:4 * pages]
    ex = rest[4 * pages:5 * pages]
    tot = rest[5 * pages:6 * pages]
    od_ref, of_ref = rest[6 * pages], rest[6 * pages + 1]
    md, ld, accd, mf, lf, accf, carry = rest[6 * pages + 2:]
    rows = 2 * DIFF_HEADS
    th = DIFF_HEADS * LANES

    def init():
        @pl.when(j == 0)
        def _():
            md[...] = jnp.full(md.shape, NEG, F32)
            mf[...] = jnp.full(mf.shape, NEG, F32)
            ld[...] = jnp.zeros(ld.shape, F32)
            lf[...] = jnp.zeros(lf.shape, F32)
            accd[...] = jnp.zeros(accd.shape, F32)
            accf[...] = jnp.zeros(accf.shape, F32)
            carry[...] = jnp.zeros(carry.shape, F32)

    def queries():
        lane1 = lax.broadcasted_iota(jnp.int32, (rows, LANES), 1)
        r1 = lax.broadcasted_iota(jnp.int32, (rows, LANES), 0)
        qd = jnp.where((lane1 >> 6) == (r1 & 1), qd_ref[0], 0.0)
        col = lax.broadcasted_iota(jnp.int32, (rows, th), 1)
        r = lax.broadcasted_iota(jnp.int32, (rows, th), 0)
        own_head = (col & (DIFF_HEADS - 1)) == (r >> 1)
        half_mask = (col >> 6) == r
        qf = jnp.where(half_mask, qf_ref[0], 0.0)
        return qd, qf, own_head, half_mask, r1

    def update(m_ref, l_ref, acc_ref, s_list, pv_fn):
        s = jnp.concatenate(s_list, axis=1)
        w = s.shape[1] // pages
        m_old = m_ref[...]
        m_new = jnp.maximum(m_old, jnp.max(s, axis=1, keepdims=True))
        alpha = jnp.exp2(m_old - m_new)
        e = jnp.exp2(s - m_new)
        l_ref[...] = alpha * l_ref[...] + jnp.sum(e, axis=1, keepdims=True)
        pv = pv_fn(e[:, 0:w], 0)
        for k in range(1, pages):
            pv = pv + pv_fn(e[:, k * w:(k + 1) * w], k)
        acc_ref[...] = alpha * acc_ref[...] + pv
        m_ref[...] = m_new

    def main():
        qd, qf, own_head, _, _ = queries()
        sd = [jnp.where(own_head, _dot_nt(qd, kd[k][...]), NEG) for k in range(pages)]
        update(md, ld, accd, sd, lambda e, k: _dot(e, vd[k][...]))

        run = carry[...]
        base = slogf_ref[0]
        sf = [None] * pages
        for k in reversed(range(pages)):
            sf[k] = _dot(qf, kf[k][...]) + LOG2E * ((base + run) + ex[k][...])
            run = run + tot[k][...]
        carry[...] = run
        update(mf, lf, accf, sf, lambda e, k: _dot_nt(e, vf[k][...]))

    def final():
        @pl.when(j == n_steps - 1)
        def _():
            qd, qf, _, half_mask, r1 = queries()
            lam = _diff_lambda(lamv_ref[...], lam_init)

            def finish(m_ref, l_ref, acc_ref, q8, k_new, v_new):
                s_new = jnp.sum(q8 * k_new, axis=1, keepdims=True)
                m_old = m_ref[...]
                m_fin = jnp.maximum(m_old, s_new)
                a = jnp.exp2(m_old - m_fin)
                en = jnp.exp2(s_new - m_fin)
                l_fin = a * l_ref[...] + en
                return (a * acc_ref[...] + en * v_new) / l_fin

            od8 = finish(md, ld, accd, qd, knd_ref[0], vnd_ref[0])
            comb = od8 * jnp.where((r1 & 1) == 0, 1.0, -lam)
            o = comb + pltpu.roll(comb, rows - 1, 0)
            ms = jnp.mean(o * o, axis=1, keepdims=True)
            od_ref[0] = o * lax.rsqrt(ms + SUBLN_EPS) * g_ref[...] * (1.0 - lam_init)
            of8 = finish(mf, lf, accf, qf, knf_ref[0], vnf_ref[0])
            of_ref[0] = jnp.sum(jnp.where(half_mask, of8, 0.0), axis=0, keepdims=True)

    return init, main, final


def _decode_kernel(pt_ref, *refs, pages, lam_init):
    del pt_ref
    for part in _decode_parts(pl.program_id(1), pl.num_programs(1), refs, pages, lam_init):
        part()


def _decode_attention(page_table, dec_args, pages, lam_init):
    db, n_pages = page_table.shape
    n_steps = n_pages // pages
    rows = FOX_HEADS

    def page_map(p):
        return lambda b, j, pt: (pt[b, (n_steps - 1 - j) * pages + p], 0, 0)

    per_seq = lambda b, j, pt: (b, 0, 0)
    const = lambda b, j, pt: (0, 0)
    r128 = pl.BlockSpec((1, rows, LANES), per_seq)
    v512 = pl.BlockSpec((1, 1, WIDTH), per_seq)
    in_specs = [r128, v512, r128, r128, v512, v512,
                pl.BlockSpec((1, FOX_HEADS, 1), per_seq),
                pl.BlockSpec((4, HEAD_DIM), const),
                pl.BlockSpec((1, LANES), const)]
    assert len(in_specs) == N_DECODE_SMALL
    for _ in range(4):
        in_specs += [pl.BlockSpec((None, WIDTH, LANES), page_map(p)) for p in range(pages)]
    for _ in range(2):
        in_specs += [pl.BlockSpec((None, FOX_HEADS, LANES), page_map(p)) for p in range(pages)]
    grid_spec = pltpu.PrefetchScalarGridSpec(
        num_scalar_prefetch=1,
        grid=(db, n_steps),
        in_specs=in_specs,
        out_specs=[r128, v512],
        scratch_shapes=[pltpu.VMEM((rows, 1), F32), pltpu.VMEM((rows, 1), F32), pltpu.VMEM((rows, LANES), F32),
                        pltpu.VMEM((rows, 1), F32), pltpu.VMEM((rows, 1), F32), pltpu.VMEM((rows, WIDTH), F32),
                        pltpu.VMEM((rows, LANES), F32)],
    )
    pools = dec_args[N_DECODE_SMALL:]
    args = list(dec_args[:N_DECODE_SMALL]) + [p for pool in pools for p in [pool] * pages]
    return pl.pallas_call(
        functools.partial(_decode_kernel, pages=pages, lam_init=lam_init),
        grid_spec=grid_spec,
        out_shape=[jax.ShapeDtypeStruct((db, rows, LANES), F32), jax.ShapeDtypeStruct((db, 1, WIDTH), F32)],
        compiler_params=_cparams(("arbitrary", "arbitrary")),
        name="decode",
    )(page_table, *args)


ROUTE_MEMBER = 3 * N_EXPERTS
GROUP_ROWS = 16


def _tail_kernel(mix_ref, x_ref, wo_ref, g_ref, b_ref, wr_ref, br_ref, h_ref, hb_ref, gate_ref, route_ref,
                 *, alpha, split):
    y = _dot(mix_ref[...].astype(BF16), wo_ref[...])
    h = _layer_norm(alpha * x_ref[...] + y, g_ref[...], b_ref[...])
    h_ref[...] = h
    hb = h.astype(BF16)
    hb_ref[...] = hb
    if split:
        two = _dot(hb, wr_ref[...])
        h_lo = (h - hb.astype(F32)).astype(BF16)
        logits = (two[:, :LANES] + two[:, LANES:]) + _dot(h_lo, wr_ref[:, :LANES]) + br_ref[...]
    else:
        logits = _dot_hi(h, wr_ref[...]) + br_ref[...]
    tm = logits.shape[0]
    lane = lax.broadcasted_iota(jnp.int32, (tm, LANES), 1).astype(F32)
    big = float(LANES)
    gmask = lane < N_GROUPS
    gl = jnp.where(gmask, logits, NEG)
    gmax = jnp.max(gl, axis=1, keepdims=True)
    gsum = jnp.sum(jnp.where(gmask, jnp.exp(gl - gmax), 0.0), axis=1, keepdims=True)
    g_val = 1.0 / gsum
    g_idx = jnp.min(jnp.where(gmask & (gl == gmax), lane, big), axis=1, keepdims=True)
    first = N_GROUPS + EXPERTS_PER_GROUP * g_idx
    emask = (lane >= first) & (lane < first + EXPERTS_PER_GROUP)
    el = jnp.where(emask, logits, NEG)
    v1 = jnp.max(el, axis=1, keepdims=True)
    i1 = jnp.min(jnp.where(emask & (el == v1), lane, big), axis=1, keepdims=True)
    el2 = jnp.where(lane == i1, NEG, el)
    v2 = jnp.max(el2, axis=1, keepdims=True)
    i2 = jnp.min(jnp.where(emask & (el2 == v2) & (lane != i1), lane, big), axis=1, keepdims=True)
    t = jnp.exp(v2 - v1)
    w1 = g_val / (1.0 + t)
    w2 = g_val * t / (1.0 + t)
    gates = jnp.where(lane == i1, w1, 0.0) + jnp.where(lane == i2, w2, 0.0)
    gate_ref[...] = gates
    hi = gates.astype(BF16).astype(F32)
    mid = (gates - hi).astype(BF16).astype(F32)
    lo = (gates - hi - mid).astype(BF16).astype(F32)
    route = (pltpu.roll(hi, LANES - N_GROUPS, 1) + pltpu.roll(mid, N_EXPERTS - N_GROUPS, 1)
             + pltpu.roll(lo, 2 * N_EXPERTS - N_GROUPS, 1))
    route = jnp.where(lane == ROUTE_MEMBER + g_idx, 1.0, route)
    route_ref[...] = route.astype(BF16)


def _tail(mix, x2d, wo_bf, g1, b1, wr, br_pad, tm, alpha):
    n = x2d.shape[0]
    row = lambda i: (i, 0)
    const = lambda i: (0, 0)
    return pl.pallas_call(
        functools.partial(_tail_kernel, alpha=alpha, split=wr.dtype == BF16),
        grid=(n // tm,),
        in_specs=[pl.BlockSpec((tm, D_MODEL), row), pl.BlockSpec((tm, D_MODEL), row),
                  pl.BlockSpec((D_MODEL, D_MODEL), const),
                  pl.BlockSpec((1, D_MODEL), const), pl.BlockSpec((1, D_MODEL), const),
                  pl.BlockSpec(wr.shape, const), pl.BlockSpec((1, LANES), const)],
        out_specs=[pl.BlockSpec((tm, D_MODEL), row), pl.BlockSpec((tm, D_MODEL), row),
                   pl.BlockSpec((tm, LANES), row), pl.BlockSpec((tm, LANES), row)],
        out_shape=[jax.ShapeDtypeStruct((n, D_MODEL), F32), jax.ShapeDtypeStruct((n, D_MODEL), BF16),
                   jax.ShapeDtypeStruct((n, LANES), F32), jax.ShapeDtypeStruct((n, LANES), BF16)],
        compiler_params=_cparams(("arbitrary",)),
        name="tail",
    )(mix, x2d, wo_bf, g1, b1, wr, br_pad)


def _swiglu_rows(x, gate, wg_ref, wu_ref, wd_ref):
    a = _dot(x, wg_ref[0])
    u = _dot(x, wu_ref[0])
    hmid = (a * jax.nn.sigmoid(a)) * u * gate
    return _dot(hmid.astype(BF16), wd_ref[0])


def _expert_specs(ix):
    return [pl.BlockSpec((1, D_MODEL, D_EXPERT), ix), pl.BlockSpec((1, D_MODEL, D_EXPERT), ix),
            pl.BlockSpec((1, D_EXPERT, D_MODEL), ix)]


def _moe_kernel(hb_ref, h_ref, gate_ref, wg_ref, wu_ref, wd_ref, g_ref, b_ref, y_ref, acc_ref, *, alpha):
    e = pl.program_id(1)

    @pl.when(e == 0)
    def _():
        acc_ref[...] = jnp.zeros(acc_ref.shape, F32)

    xb = hb_ref[...]
    lane = lax.broadcasted_iota(jnp.int32, (xb.shape[0], LANES), 1)
    gate = jnp.sum(jnp.where(lane == e + N_GROUPS, gate_ref[...], 0.0), axis=1, keepdims=True)
    acc_ref[...] += _swiglu_rows(xb, gate, wg_ref, wu_ref, wd_ref)

    @pl.when(e == N_EXPERTS - 1)
    def _():
        y_ref[...] = _layer_norm(alpha * h_ref[...] + acc_ref[...], g_ref[...], b_ref[...])


def _moe(hb, h, gates, wg_bf, wu_bf, wd_bf, g2, b2, tm, alpha):
    n = h.shape[0]
    row = lambda i, e: (i, 0)
    const = lambda i, e: (0, 0)
    return pl.pallas_call(
        functools.partial(_moe_kernel, alpha=alpha),
        grid=(n // tm, N_EXPERTS),
        in_specs=[pl.BlockSpec((tm, D_MODEL), row), pl.BlockSpec((tm, D_MODEL), row),
                  pl.BlockSpec((tm, LANES), row)] + _expert_specs(lambda i, e: (e, 0, 0))
                 + [pl.BlockSpec((1, D_MODEL), const), pl.BlockSpec((1, D_MODEL), const)],
        out_specs=pl.BlockSpec((tm, D_MODEL), row),
        out_shape=jax.ShapeDtypeStruct((n, D_MODEL), F32),
        scratch_shapes=[pltpu.VMEM((tm, D_MODEL), F32)],
        compiler_params=_cparams(("arbitrary", "arbitrary")),
        name="moe",
    )(hb, h, gates, wg_bf, wu_bf, wd_bf, g2, b2)


def _route_gate(route, lane, e):
    pick = (lane == e) | (lane == e + N_EXPERTS) | (lane == e + 2 * N_EXPERTS)
    return jnp.sum(jnp.where(pick, route, 0.0), axis=1, keepdims=True)


def _moe_grouped_kernel(hb_ref, h_ref, route_ref, wg_ref, wu_ref, wd_ref, g_ref, b_ref, y_ref,
                        acc_ref, rcol_ref, rrow_ref, mrow_ref, xg_ref, gsel_ref, yg_ref, cnt_ref,
                        *, alpha, cap):
    e = pl.program_id(1)
    grp = e // EXPERTS_PER_GROUP
    ts = hb_ref.shape[0]
    cap_pad = yg_ref.shape[0]
    lane = lax.broadcasted_iota(jnp.int32, (ts, LANES), 1)

    @pl.when(e == 0)
    def _():
        acc_ref[...] = jnp.zeros(acc_ref.shape, F32)
        t = lax.broadcasted_iota(jnp.int32, (LANES, LANES), 0)
        u = lax.broadcasted_iota(jnp.int32, (LANES, LANES), 1)
        lower = (u < t).astype(BF16)
        carry_c = jnp.zeros((1, LANES), F32)
        carry_r = jnp.zeros((GROUP_ROWS, 1), F32)
        for blk in range(ts // LANES):
            sl = slice(blk * LANES, (blk + 1) * LANES)
            memb = jnp.where((u >= ROUTE_MEMBER) & (u < ROUTE_MEMBER + N_GROUPS),
                             route_ref[sl, :].astype(F32), 0.0)
            rcol_ref[sl, :] = _dot(lower, memb.astype(BF16)) + carry_c
            carry_c = carry_c + jnp.sum(memb, axis=0, keepdims=True)
            mrow = memb.T[ROUTE_MEMBER:ROUTE_MEMBER + GROUP_ROWS, :]
            mrow_ref[:, sl] = mrow
            rrow_ref[:, sl] = _dot_nt(mrow.astype(BF16), lower) + carry_r
            carry_r = carry_r + jnp.sum(mrow, axis=1, keepdims=True)
        lane1 = lax.broadcasted_iota(jnp.int32, (1, LANES), 1)
        for gi in range(N_GROUPS):
            cnt_ref[gi] = jnp.sum(jnp.where(lane1 == ROUTE_MEMBER + gi, carry_c, 0.0)).astype(jnp.int32)

    small = cnt_ref[grp] <= cap

    @pl.when((e % EXPERTS_PER_GROUP == 0) & small)
    def _():
        pos = lax.broadcasted_iota(jnp.int32, (cap, ts), 0).astype(F32)
        rank = jnp.where(mrow_ref[pl.ds(grp, 1), :] > 0.5, rrow_ref[pl.ds(grp, 1), :], -1.0)
        onehot = jnp.where(rank == pos, 1.0, 0.0).astype(BF16)
        xg_ref[...] = _dot(onehot, hb_ref[...]).astype(BF16)
        gsel_ref[...] = _dot(onehot, route_ref[...])
        yg_ref[...] = jnp.zeros(yg_ref.shape, F32)

    @pl.when(small)
    def _():
        lane_c = lax.broadcasted_iota(jnp.int32, (cap, LANES), 1)
        gate = _route_gate(gsel_ref[...], lane_c, e)
        yg_ref[0:cap, :] += _swiglu_rows(xg_ref[...], gate, wg_ref, wu_ref, wd_ref)

    @pl.when(jnp.logical_not(small))
    def _():
        gate = _route_gate(route_ref[...].astype(F32), lane, e)
        acc_ref[...] += _swiglu_rows(hb_ref[...], gate, wg_ref, wu_ref, wd_ref)

    @pl.when((e % EXPERTS_PER_GROUP == EXPERTS_PER_GROUP - 1) & small)
    def _():
        mine = lane == ROUTE_MEMBER + grp
        rank = jnp.sum(jnp.where(mine, rcol_ref[...], 0.0), axis=1, keepdims=True)
        member = jnp.sum(jnp.where(mine, route_ref[...].astype(F32), 0.0), axis=1, keepdims=True)
        pos = lax.broadcasted_iota(jnp.int32, (ts, cap_pad), 1).astype(F32)
        rank = jnp.where(member > 0.5, rank, -1.0)
        onehot_t = jnp.where(rank == pos, 1.0, 0.0).astype(BF16)
        acc_ref[...] += _dot(onehot_t, yg_ref[...].astype(BF16))

    @pl.when(e == N_EXPERTS - 1)
    def _():
        y_ref[...] = _layer_norm(alpha * h_ref[...] + acc_ref[...], g_ref[...], b_ref[...])


def _moe_grouped(hb, h, route, wg_bf, wu_bf, wd_bf, g2, b2, ts, cap, alpha):
    n = h.shape[0]
    cap_pad = -(-cap // LANES) * LANES
    row = lambda i, e: (i, 0)
    const = lambda i, e: (0, 0)
    return pl.pallas_call(
        functools.partial(_moe_grouped_kernel, alpha=alpha, cap=cap),
        grid=(n // ts, N_EXPERTS),
        in_specs=[pl.BlockSpec((ts, D_MODEL), row), pl.BlockSpec((ts, D_MODEL), row),
                  pl.BlockSpec((ts, LANES), row)] + _expert_specs(lambda i, e: (e, 0, 0))
                 + [pl.BlockSpec((1, D_MODEL), const), pl.BlockSpec((1, D_MODEL), const)],
        out_specs=pl.BlockSpec((ts, D_MODEL), row),
        out_shape=jax.ShapeDtypeStruct((n, D_MODEL), F32),
        scratch_shapes=[pltpu.VMEM((ts, D_MODEL), F32),
                        pltpu.VMEM((ts, LANES), F32),
                        pltpu.VMEM((GROUP_ROWS, ts), F32),
                        pltpu.VMEM((GROUP_ROWS, ts), F32),
                        pltpu.VMEM((cap, D_MODEL), BF16),
                        pltpu.VMEM((cap, LANES), F32),
                        pltpu.VMEM((cap_pad, D_MODEL), F32),
                        pltpu.SMEM((N_GROUPS,), jnp.int32)],
        compiler_params=_cparams(("arbitrary", "arbitrary")),
        name="moe_grouped",
    )(hb, h, route, wg_bf, wu_bf, wd_bf, g2, b2)


def _rope_tables(pos):
    half = HEAD_DIM // 2
    inv = ROPE_THETA ** (-jnp.arange(half, dtype=F32) * 2.0 / HEAD_DIM)
    ang = pos.astype(F32)[:, None] * inv[None, :]
    cos, sin = jnp.cos(ang), jnp.sin(ang)
    reps = WIDTH // HEAD_DIM
    cos_t = jnp.tile(jnp.concatenate([cos, cos], axis=1), (1, reps))
    sin_t = jnp.tile(jnp.concatenate([-sin, sin], axis=1), (1, reps))
    return cos_t, sin_t


def kernel(x_prompt, x_sample, cache_diff_k, cache_diff_v, cache_fox_k, cache_fox_v, cache_fox_logf, page_table,
           w_in, b_forget, lambda_q1, lambda_k1, lambda_q2, lambda_k2, subln_g, w_out, ln1_g, ln1_b,
           w_router_group, b_router_group, w_router_expert, b_router_expert, w_gate, w_up, w_down, ln2_g, ln2_b):
    B, S, D = x_prompt.shape
    DB, T, _ = x_sample.shape
    depth = w_in.shape[0]
    assert depth == 1 and T == 1 and D == D_MODEL
    n_pool, page = cache_diff_k.shape[1], cache_diff_k.shape[2]
    n_pages = page_table.shape[1]
    past_len = n_pages * page
    assert page == LANES
    alpha = (2 * depth) ** 0.25
    l = 0
    lam_init = 0.8 - 0.6 * math.exp(-0.3 * l)

    w_pad = jnp.pad(w_in[l], ((0, 0), (0, 6 * WIDTH + LANES - w_in.shape[2]))).astype(BF16)
    bf_pad = jnp.pad(b_forget[l], (0, LANES - FOX_HEADS)).reshape(1, LANES)
    lamv = jnp.stack([lambda_q1[l], lambda_k1[l], lambda_q2[l], lambda_k2[l]])
    g1 = subln_g[l].reshape(1, LANES)
    wo_bf = w_out[l].astype(BF16)
    wr_pad = jnp.pad(jnp.concatenate([w_router_group[l], w_router_expert[l]], axis=1),
                     ((0, 0), (0, LANES - N_GROUPS - N_EXPERTS)))
    br_pad = jnp.pad(jnp.concatenate([b_router_group[l], b_router_expert[l]]),
                     (0, LANES - N_GROUPS - N_EXPERTS)).reshape(1, LANES)
    wg_bf, wu_bf, wd_bf = w_gate[l].astype(BF16), w_up[l].astype(BF16), w_down[l].astype(BF16)
    ln1g, ln1b = ln1_g[l].reshape(1, D), ln1_b[l].reshape(1, D)
    ln2g, ln2b = ln2_g[l].reshape(1, D), ln2_b[l].reshape(1, D)
    p_cos, p_sin = _rope_tables(jnp.arange(S, dtype=jnp.int32))
    s_cos, s_sin = _rope_tables(jnp.full((DB,), past_len, dtype=jnp.int32))

    tm = 256
    xp2 = x_prompt.reshape(B * S, D)
    (dq, dkb, dvb, fq, fkb, fvb, dk, dv, fk, fv, logf) = _project(xp2, w_pad, p_cos, p_sin, bf_pad, tm, S // tm)
    logf_t = jnp.swapaxes(logf.reshape(B, S, FOX_HEADS), 1, 2).reshape(B * FOX_HEADS, S)
    c_t = _cumsum_rows(logf_t).reshape(B, FOX_HEADS, S)
    tq = 512
    c = jnp.swapaxes(c_t, 1, 2)
    c_blk = jnp.swapaxes(c_t.reshape(B, FOX_HEADS, S // tq, tq), 1, 2)
    r3 = lambda a: a.reshape(B, S, WIDTH)
    mix_p = _prompt_attention(r3(dq), r3(fq), r3(dkb), r3(dvb), r3(fkb), r3(fvb), c, c_blk, lamv, g1,
                              tq, lam_init, 1)
    wr_hi = wr_pad.astype(BF16)
    wr_split = jnp.concatenate([wr_hi, (wr_pad - wr_hi.astype(F32)).astype(BF16)], axis=1)
    h_p, hb_p, _, route_p = _tail(mix_p.reshape(B * S, D), xp2, wo_bf, ln1g, ln1b, wr_split, br_pad, 256, alpha)
    y_p = _moe_grouped(hb_p, h_p, route_p, wg_bf, wu_bf, wd_bf, ln2g, ln2b, MOE_TILE, MOE_GROUP_CAP, alpha)

    xs2 = x_sample.reshape(DB, D)
    (sdq, _, _, sfq, _, _, sdk, sdv, sfk, sfv, slogf) = _project(xs2, w_pad, s_cos, s_sin, bf_pad, DB, 1)
    diff_pool = lambda a: a[l].reshape(n_pool, page * DIFF_HEADS, 2 * HEAD_DIM)
    fox_pool = lambda a: jnp.transpose(a[l], (0, 2, 3, 1)).reshape(n_pool, FOX_HEADS * HEAD_DIM, page)
    lft = jnp.transpose(cache_fox_logf[l], (0, 2, 1)).reshape(n_pool * FOX_HEADS, page)
    ex, tot = _page_suffix(lft, 4096)
    rep = lambda a: jnp.repeat(a.astype(F32).reshape(DB, DIFF_HEADS, 2 * HEAD_DIM), 2, axis=1)
    v3 = lambda a: a.astype(F32).reshape(DB, 1, WIDTH)
    dec_args = [rep(sdq), v3(sfq), rep(sdk), rep(sdv), v3(sfk), v3(sfv), slogf.reshape(DB, FOX_HEADS, 1), lamv, g1,
                diff_pool(cache_diff_k), diff_pool(cache_diff_v), fox_pool(cache_fox_k), fox_pool(cache_fox_v),
                ex.reshape(n_pool, FOX_HEADS, page), tot.reshape(n_pool, FOX_HEADS, page)]
    od_s, of_s = _decode_attention(page_table, dec_args, DECODE_PAGES, lam_init)
    mix_s = jnp.concatenate([od_s[:, 0::2, :].reshape(DB, WIDTH), of_s.reshape(DB, WIDTH)], axis=1)
    h_s, hb_s, gates_s, _ = _tail(mix_s, xs2, wo_bf, ln1g, ln1b, wr_pad, br_pad, DB, alpha)
    y_s = _moe(hb_s, h_s, gates_s, wg_bf, wu_bf, wd_bf, ln2g, ln2b, DB, alpha)

    return (y_p.reshape(B, S, D), y_s.reshape(DB, T, D),
            dk.reshape(1, B, S, DIFF_HEADS, 2 * HEAD_DIM), dv.reshape(1, B, S, DIFF_HEADS, 2 * HEAD_DIM),
            fk.reshape(1, B, S, FOX_HEADS, HEAD_DIM), fv.reshape(1, B, S, FOX_HEADS, HEAD_DIM),
            logf.reshape(1, B, S, FOX_HEADS),
            sdk.reshape(1, DB, T, DIFF_HEADS, 2 * HEAD_DIM), sdv.reshape(1, DB, T, DIFF_HEADS, 2 * HEAD_DIM),
            sfk.reshape(1, DB, T, FOX_HEADS, HEAD_DIM), sfv.reshape(1, DB, T, FOX_HEADS, HEAD_DIM),
            slogf.reshape(1, DB, T, FOX_HEADS))
```

```python
import functools
import math

import jax
import jax.numpy as jnp
from jax import lax
from jax.experimental import pallas as pl
from jax.experimental.pallas import tpu as pltpu

F32 = jnp.float32
BF16 = jnp.bfloat16

HEAD_DIM = 64
DIFF_HEADS = 4
FOX_HEADS = 8
WIDTH = 512
D_MODEL = 1024
N_GROUPS = 4
EXPERTS_PER_GROUP = 4
N_EXPERTS = 16
D_EXPERT = 512
ROPE_THETA = 10000.0
LN_EPS = 1e-5
SUBLN_EPS = 1e-5
LANES = 128
NEG = -1e30
LOG2E = 1.4426950408889634
V7X_VMEM_LIMIT = 48 * 1024 * 1024
ATTN_TILE = 512
DECODE_PAGES = 16
MOE_TILE = 1024
MOE_GROUP_CAP = 320


def _cparams(sem):
    return pltpu.CompilerParams(dimension_semantics=sem, vmem_limit_bytes=V7X_VMEM_LIMIT)


def _dot(a, b):
    return jnp.dot(a, b, preferred_element_type=F32)


def _dot_nt(a, b):
    return lax.dot_general(a, b, (((1,), (1,)), ((), ())), preferred_element_type=F32)


def _dot_hi(a, b):
    return jnp.dot(a, b, preferred_element_type=F32, precision=lax.Precision.HIGHEST)


def _layer_norm(x, g, b):
    mu = jnp.mean(x, axis=-1, keepdims=True)
    xc = x - mu
    var = jnp.mean(xc * xc, axis=-1, keepdims=True)
    return xc * lax.rsqrt(var + LN_EPS) * g + b


def _diff_lambda(lv, lam_init):
    return (jnp.exp(jnp.sum(lv[0:1] * lv[1:2], axis=1, keepdims=True))
            - jnp.exp(jnp.sum(lv[2:3] * lv[3:4], axis=1, keepdims=True)) + lam_init)


def _proj_kernel(x_ref, w_ref, cos_ref, sin_ref, bf_ref,
                 dq_ref, dkb_ref, fq_ref, fkb_ref, dk_ref, dv_ref, fk_ref, fv_ref, logf_ref, *vt_refs):
    tm = x_ref.shape[0]
    xb = x_ref[...].astype(BF16)
    cos = cos_ref[...]
    sin = sin_ref[...]
    lane = lax.broadcasted_iota(jnp.int32, (tm, WIDTH), 1)
    first_half = (lane & (HEAD_DIM // 2)) == 0
    qscale = HEAD_DIM ** -0.5 * LOG2E

    def mm(c):
        return _dot(xb, w_ref[:, c * WIDTH:(c + 1) * WIDTH])

    def rope(p):
        partner = jnp.where(first_half,
                            pltpu.roll(p, WIDTH - HEAD_DIM // 2, 1),
                            pltpu.roll(p, HEAD_DIM // 2, 1))
        return p * cos + partner * sin

    dq = rope(mm(0))
    dq_ref[...] = (dq * qscale).astype(BF16)
    dk = rope(mm(1))
    for hh in range(DIFF_HEADS):
        dk_ref[:, hh, :] = dk[:, hh * LANES:(hh + 1) * LANES]
    dkb_ref[...] = dk.astype(BF16)
    dv = mm(2)
    for hh in range(DIFF_HEADS):
        dv_ref[:, hh, :] = dv[:, hh * LANES:(hh + 1) * LANES]
    fq_ref[...] = (mm(3) * qscale).astype(BF16)
    fk = mm(4)
    fkb_ref[...] = fk.astype(BF16)
    fv = mm(5)
    if vt_refs:
        fvt = fv.T
        fk_ref[0] = fk.T
        fv_ref[0] = fvt
        vt_refs[0][0] = dv.T.astype(BF16)
        vt_refs[1][0] = fvt.astype(BF16)
    else:
        fk_ref[...] = fk
        fv_ref[...] = fv
    z = _dot(xb, w_ref[:, 6 * WIDTH:6 * WIDTH + LANES]) + bf_ref[...]
    logf = jnp.minimum(z, 0.0) - jnp.log1p(jnp.exp(-jnp.abs(z)))
    logf_ref[...] = logf[:, :FOX_HEADS]


def _project(x2d, w_bf, cos_t, sin_t, bf_pad, tm, table_blocks, prompt_layout):
    n = x2d.shape[0]
    grid = (n // tm,)
    row = lambda i: (i, 0)
    tab = lambda i: (i % table_blocks, 0)
    const = lambda i: (0, 0)
    wide = pl.BlockSpec((tm, WIDTH), row)
    out_shape = ([jax.ShapeDtypeStruct((n, WIDTH), BF16)] * 4
                 + [jax.ShapeDtypeStruct((n, DIFF_HEADS, LANES), F32)] * 2
                 + [jax.ShapeDtypeStruct((n, WIDTH), F32)] * 2
                 + [jax.ShapeDtypeStruct((n, FOX_HEADS), F32)])
    head3 = pl.BlockSpec((tm, DIFF_HEADS, LANES), lambda i: (i, 0, 0))
    out_specs = [wide] * 4 + [head3, head3, wide, wide] + [pl.BlockSpec((tm, FOX_HEADS), row)]
    if prompt_layout:
        n_seq = n // (table_blocks * tm)
        by_seq = pl.BlockSpec((1, WIDTH, tm), lambda i: (i // table_blocks, 0, i % table_blocks))
        out_shape[6:8] = [jax.ShapeDtypeStruct((n_seq, WIDTH, table_blocks * tm), F32)] * 2
        out_specs[6:8] = [by_seq, by_seq]
        out_shape += [jax.ShapeDtypeStruct((n // tm, WIDTH, tm), BF16)] * 2
        out_specs += [pl.BlockSpec((1, WIDTH, tm), lambda i: (i, 0, 0))] * 2
    return pl.pallas_call(
        _proj_kernel,
        grid=grid,
        in_specs=[pl.BlockSpec((tm, D_MODEL), row),
                  pl.BlockSpec(w_bf.shape, const),
                  pl.BlockSpec((tm, WIDTH), tab),
                  pl.BlockSpec((tm, WIDTH), tab),
                  pl.BlockSpec((1, LANES), const)],
        out_specs=out_specs,
        out_shape=out_shape,
        compiler_params=_cparams(("arbitrary",)),
        name="proj",
    )(x2d, w_bf, cos_t, sin_t, bf_pad)


def _cumsum_kernel(x_ref, o_ref):
    rows, s = x_ref.shape
    t = lax.broadcasted_iota(jnp.int32, (LANES, LANES), 0)
    u = lax.broadcasted_iota(jnp.int32, (LANES, LANES), 1)
    tri = (t <= u).astype(F32)
    carry = jnp.zeros((rows, 1), F32)
    for ch in range(s // LANES):
        blk = _dot_hi(x_ref[:, ch * LANES:(ch + 1) * LANES], tri) + carry
        o_ref[:, ch * LANES:(ch + 1) * LANES] = blk * LOG2E
        carry = blk[:, LANES - 1:LANES]


def _cumsum_rows(xt):
    return pl.pallas_call(
        _cumsum_kernel,
        out_shape=jax.ShapeDtypeStruct(xt.shape, F32),
        name="cumsum",
    )(xt)


def _attn_kernel(dq_ref, fq_ref, dk_ref, fk_ref, dvt_ref, fvt_ref, c_ref, cq_ref, lamv_ref, g_ref,
                 mix_ref, *, tq, lam_init):
    qi = pl.program_id(1)
    tk = tq
    lane = lax.broadcasted_iota(jnp.int32, (tq, LANES), 1)
    lo = lane < HEAD_DIM
    key_i = lax.broadcasted_iota(jnp.int32, (tk, tq), 0)
    qry_i = lax.broadcasted_iota(jnp.int32, (tk, tq), 1)
    causal = key_i <= qry_i
    top = lax.broadcasted_iota(jnp.int32, (LANES, tq), 0) < HEAD_DIM
    lam = _diff_lambda(lamv_ref[...], lam_init)
    g = g_ref[...]

    for u in range(DIFF_HEADS + FOX_HEADS // 2):
        is_diff = u < DIFF_HEADS
        off = (u if is_diff else u - DIFF_HEADS) * LANES
        k_ref, vt_ref = (dk_ref, dvt_ref) if is_diff else (fk_ref, fvt_ref)
        q_pair = (dq_ref if is_diff else fq_ref)[0, :, off:off + LANES]
        zero = jnp.zeros_like(q_pair)
        qs = (jnp.where(lo, q_pair, zero), jnp.where(lo, zero, q_pair))
        h0 = None if is_diff else 2 * (u - DIFF_HEADS)

        def qk(j, off=off, k_ref=k_ref, qs=qs):
            k = k_ref[0, pl.ds(pl.multiple_of(j * tk, tk), tk), off:off + LANES]
            return tuple(_dot_nt(k, q) for q in qs)

        def step(j, carry, scores, masked, off=off, vt_ref=vt_ref, h0=h0):
            start = pl.multiple_of(j * tk, tk)
            vt = vt_ref[0, j, off:off + LANES, :]
            out = []
            for idx in range(2):
                m, l, acc = carry[idx]
                st = scores[idx]
                if h0 is None:
                    cq = None
                else:
                    st = st - c_ref[0, pl.ds(start, tk), h0 + idx:h0 + idx + 1]
                    cq = cq_ref[0, 0, h0 + idx:h0 + idx + 1, :]
                if masked:
                    st = jnp.where(causal, st, NEG)
                mt = jnp.max(st, axis=0, keepdims=True)
                m_new = jnp.maximum(m, mt if cq is None else mt + cq)
                alpha = jnp.exp2(m - m_new)
                e = jnp.exp2(st - (m_new if cq is None else m_new - cq))
                l = alpha * l + jnp.sum(e, axis=0, keepdims=True)
                acc = alpha * acc + _dot(vt, e.astype(BF16))
                out.append((m_new, l, acc))
            return tuple(out)

        init = tuple((jnp.full((1, tq), NEG, F32), jnp.zeros((1, tq), F32), jnp.zeros((LANES, tq), F32))
                     for _ in range(2))
        carry = lax.fori_loop(0, qi, lambda j, c: step(j, c, qk(j), False), init)
        (_, la, acca), (_, lb, accb) = step(qi, carry, qk(qi), True)
        if is_diff:
            o = (acca / la - lam * (accb / lb)).T
            ms = jnp.mean(o * o, axis=1, keepdims=True)
            o = o * lax.rsqrt(ms + SUBLN_EPS) * g * (1.0 - lam_init)
            mix_ref[0, :, off:off + LANES] = o.astype(BF16)
        else:
            o = jnp.where(top, acca / la, accb / lb).T
            mix_ref[0, :, WIDTH + off:WIDTH + off + LANES] = o.astype(BF16)


def _prompt_attention(dq, fq, dk, fk, dvt, fvt, c, cq, lamv, g, tq, lam_init):
    b, s, _ = dq.shape
    qspec = pl.BlockSpec((1, tq, WIDTH), lambda bi, qi: (bi, qi, 0))
    kspec = pl.BlockSpec((1, s, WIDTH), lambda bi, qi: (bi, 0, 0))
    vspec = pl.BlockSpec((1, s // tq, WIDTH, tq), lambda bi, qi: (bi, 0, 0, 0))
    return pl.pallas_call(
        functools.partial(_attn_kernel, tq=tq, lam_init=lam_init),
        grid=(b, s // tq),
        in_specs=[qspec, qspec, kspec, kspec, vspec, vspec,
                  pl.BlockSpec((1, s, FOX_HEADS), lambda bi, qi: (bi, 0, 0)),
                  pl.BlockSpec((1, 1, FOX_HEADS, tq), lambda bi, qi: (bi, qi, 0, 0)),
                  pl.BlockSpec((4, HEAD_DIM), lambda bi, qi: (0, 0)),
                  pl.BlockSpec((1, LANES), lambda bi, qi: (0, 0))],
        out_specs=pl.BlockSpec((1, tq, D_MODEL), lambda bi, qi: (bi, qi, 0)),
        out_shape=jax.ShapeDtypeStruct((b, s, D_MODEL), BF16),
        compiler_params=_cparams(("arbitrary", "arbitrary")),
        name="attn",
    )(dq, fq, dk, fk, dvt, fvt, c, cq, lamv, g)


def _suffix_kernel(x_ref, e_ref, t_ref):
    t = lax.broadcasted_iota(jnp.int32, (LANES, LANES), 0)
    u = lax.broadcasted_iota(jnp.int32, (LANES, LANES), 1)
    x = x_ref[...]
    e_ref[...] = _dot_hi(x, (t > u).astype(F32))
    t_ref[...] = _dot_hi(x, jnp.ones((LANES, LANES), F32))


def _page_suffix(lft, tr):
    n = lft.shape[0]
    spec = pl.BlockSpec((tr, LANES), lambda i: (i, 0))
    return pl.pallas_call(
        _suffix_kernel,
        grid=(n // tr,),
        in_specs=[spec],
        out_specs=[spec, spec],
        out_shape=[jax.ShapeDtypeStruct(lft.shape, F32)] * 2,
        compiler_params=_cparams(("arbitrary",)),
        name="suffix",
    )(lft)


N_DECODE_SMALL = 9


def _decode_parts(j, n_steps, refs, pages, lam_init):
    qd_ref, qf_ref, knd_ref, vnd_ref, knf_ref, vnf_ref, slogf_ref, lamv_ref, g_ref = refs[:N_DECODE_SMALL]
    rest = refs[N_DECODE_SMALL:]
    kd = rest[0:pages]
    vd = rest[pages:2 * pages]
    kf = rest[2 * pages:3 * pages]
    vf = rest[3 * pages:4 * pages]
    ex = rest[4 * pages:5 * pages]
    tot = rest[5 * pages:6 * pages]
    od_ref, of_ref = rest[6 * pages], rest[6 * pages + 1]
    md, ld, accd, mf, lf, accf, carry = rest[6 * pages + 2:]
    rows = 2 * DIFF_HEADS
    th = DIFF_HEADS * LANES

    def init():
        @pl.when(j == 0)
        def _():
            md[...] = jnp.full(md.shape, NEG, F32)
            mf[...] = jnp.full(mf.shape, NEG, F32)
            ld[...] = jnp.zeros(ld.shape, F32)
            lf[...] = jnp.zeros(lf.shape, F32)
            accd[...] = jnp.zeros(accd.shape, F32)
            accf[...] = jnp.zeros(accf.shape, F32)
            carry[...] = jnp.zeros(carry.shape, F32)

    def queries():
        lane1 = lax.broadcasted_iota(jnp.int32, (rows, LANES), 1)
        r1 = lax.broadcasted_iota(jnp.int32, (rows, LANES), 0)
        qd = jnp.where((lane1 >> 6) == (r1 & 1), qd_ref[0], 0.0)
        col = lax.broadcasted_iota(jnp.int32, (rows, th), 1)
        r = lax.broadcasted_iota(jnp.int32, (rows, th), 0)
        own_head = (col & (DIFF_HEADS - 1)) == (r >> 1)
        half_mask = (col >> 6) == r
        qf = jnp.where(half_mask, qf_ref[0], 0.0)
        return qd, qf, own_head, half_mask, r1

    def update(m_ref, l_ref, acc_ref, s_list, pv_fn):
        s = jnp.concatenate(s_list, axis=1)
        w = s.shape[1] // pages
        m_old = m_ref[...]
        m_new = jnp.maximum(m_old, jnp.max(s, axis=1, keepdims=True))
        alpha = jnp.exp2(m_old - m_new)
        e = jnp.exp2(s - m_new)
        l_ref[...] = alpha * l_ref[...] + jnp.sum(e, axis=1, keepdims=True)
        pv = pv_fn(e[:, 0:w], 0)
        for k in range(1, pages):
            pv = pv + pv_fn(e[:, k * w:(k + 1) * w], k)
        acc_ref[...] = alpha * acc_ref[...] + pv
        m_ref[...] = m_new

    def main():
        qd, qf, own_head, _, _ = queries()
        sd = [jnp.where(own_head, _dot_nt(qd, kd[k][...]), NEG) for k in range(pages)]
        update(md, ld, accd, sd, lambda e, k: _dot(e, vd[k][...]))

        run = carry[...]
        base = slogf_ref[0]
        sf = [None] * pages
        for k in reversed(range(pages)):
            sf[k] = _dot(qf, kf[k][...]) + LOG2E * ((base + run) + ex[k][...])
            run = run + tot[k][...]
        carry[...] = run
        update(mf, lf, accf, sf, lambda e, k: _dot_nt(e, vf[k][...]))

    def final():
        @pl.when(j == n_steps - 1)
        def _():
            qd, qf, _, half_mask, r1 = queries()
            lam = _diff_lambda(lamv_ref[...], lam_init)

            def finish(m_ref, l_ref, acc_ref, q8, k_new, v_new):
                s_new = jnp.sum(q8 * k_new, axis=1, keepdims=True)
                m_old = m_ref[...]
                m_fin = jnp.maximum(m_old, s_new)
                a = jnp.exp2(m_old - m_fin)
                en = jnp.exp2(s_new - m_fin)
                l_fin = a * l_ref[...] + en
                return (a * acc_ref[...] + en * v_new) / l_fin

            od8 = finish(md, ld, accd, qd, knd_ref[0], vnd_ref[0])
            comb = od8 * jnp.where((r1 & 1) == 0, 1.0, -lam)
            o = comb + pltpu.roll(comb, rows - 1, 0)
            ms = jnp.mean(o * o, axis=1, keepdims=True)
            od_ref[0] = o * lax.rsqrt(ms + SUBLN_EPS) * g_ref[...] * (1.0 - lam_init)
            of8 = finish(mf, lf, accf, qf, knf_ref[0], vnf_ref[0])
            of_ref[0] = jnp.sum(jnp.where(half_mask, of8, 0.0), axis=0, keepdims=True)

    return init, main, final


def _decode_kernel(pt_ref, *refs, pages, lam_init):
    del pt_ref
    for part in _decode_parts(pl.program_id(1), pl.num_programs(1), refs, pages, lam_init):
        part()


def _decode_attention(page_table, dec_args, pages, lam_init):
    db, n_pages = page_table.shape
    n_steps = n_pages // pages
    rows = FOX_HEADS

    def page_map(p):
        return lambda b, j, pt: (pt[b, (n_steps - 1 - j) * pages + p], 0, 0)

    per_seq = lambda b, j, pt: (b, 0, 0)
    const = lambda b, j, pt: (0, 0)
    r128 = pl.BlockSpec((1, rows, LANES), per_seq)
    v512 = pl.BlockSpec((1, 1, WIDTH), per_seq)
    in_specs = [r128, v512, r128, r128, v512, v512,
                pl.BlockSpec((1, FOX_HEADS, 1), per_seq),
                pl.BlockSpec((4, HEAD_DIM), const),
                pl.BlockSpec((1, LANES), const)]
    assert len(in_specs) == N_DECODE_SMALL
    for _ in range(4):
        in_specs += [pl.BlockSpec((None, WIDTH, LANES), page_map(p)) for p in range(pages)]
    for _ in range(2):
        in_specs += [pl.BlockSpec((None, FOX_HEADS, LANES), page_map(p)) for p in range(pages)]
    grid_spec = pltpu.PrefetchScalarGridSpec(
        num_scalar_prefetch=1,
        grid=(db, n_steps),
        in_specs=in_specs,
        out_specs=[r128, v512],
        scratch_shapes=[pltpu.VMEM((rows, 1), F32), pltpu.VMEM((rows, 1), F32), pltpu.VMEM((rows, LANES), F32),
                        pltpu.VMEM((rows, 1), F32), pltpu.VMEM((rows, 1), F32), pltpu.VMEM((rows, WIDTH), F32),
                        pltpu.VMEM((rows, LANES), F32)],
    )
    pools = dec_args[N_DECODE_SMALL:]
    args = list(dec_args[:N_DECODE_SMALL]) + [p for pool in pools for p in [pool] * pages]
    return pl.pallas_call(
        functools.partial(_decode_kernel, pages=pages, lam_init=lam_init),
        grid_spec=grid_spec,
        out_shape=[jax.ShapeDtypeStruct((db, rows, LANES), F32), jax.ShapeDtypeStruct((db, 1, WIDTH), F32)],
        compiler_params=_cparams(("arbitrary", "arbitrary")),
        name="decode",
    )(page_table, *args)


ROUTE_MEMBER = 3 * N_EXPERTS
GROUP_ROWS = 16


def _tail_kernel(mix_ref, x_ref, wo_ref, g_ref, b_ref, wr_ref, br_ref, h_ref, hb_ref, gate_ref, route_ref,
                 *, alpha, split):
    y = _dot(mix_ref[...].astype(BF16), wo_ref[...])
    h = _layer_norm(alpha * x_ref[...] + y, g_ref[...], b_ref[...])
    h_ref[...] = h
    hb = h.astype(BF16)
    hb_ref[...] = hb
    if split:
        two = _dot(hb, wr_ref[...])
        h_lo = (h - hb.astype(F32)).astype(BF16)
        logits = (two[:, :LANES] + two[:, LANES:]) + _dot(h_lo, wr_ref[:, :LANES]) + br_ref[...]
    else:
        logits = _dot_hi(h, wr_ref[...]) + br_ref[...]
    tm = logits.shape[0]
    lane = lax.broadcasted_iota(jnp.int32, (tm, LANES), 1).astype(F32)
    big = float(LANES)
    gmask = lane < N_GROUPS
    gl = jnp.where(gmask, logits, NEG)
    gmax = jnp.max(gl, axis=1, keepdims=True)
    gsum = jnp.sum(jnp.where(gmask, jnp.exp(gl - gmax), 0.0), axis=1, keepdims=True)
    g_val = 1.0 / gsum
    g_idx = jnp.min(jnp.where(gmask & (gl == gmax), lane, big), axis=1, keepdims=True)
    first = N_GROUPS + EXPERTS_PER_GROUP * g_idx
    emask = (lane >= first) & (lane < first + EXPERTS_PER_GROUP)
    el = jnp.where(emask, logits, NEG)
    v1 = jnp.max(el, axis=1, keepdims=True)
    i1 = jnp.min(jnp.where(emask & (el == v1), lane, big), axis=1, keepdims=True)
    el2 = jnp.where(lane == i1, NEG, el)
    v2 = jnp.max(el2, axis=1, keepdims=True)
    i2 = jnp.min(jnp.where(emask & (el2 == v2) & (lane != i1), lane, big), axis=1, keepdims=True)
    t = jnp.exp(v2 - v1)
    w1 = g_val / (1.0 + t)
    w2 = g_val * t / (1.0 + t)
    gates = jnp.where(lane == i1, w1, 0.0) + jnp.where(lane == i2, w2, 0.0)
    gate_ref[...] = gates
    hi = gates.astype(BF16).astype(F32)
    mid = (gates - hi).astype(BF16).astype(F32)
    lo = (gates - hi - mid).astype(BF16).astype(F32)
    route = (pltpu.roll(hi, LANES - N_GROUPS, 1) + pltpu.roll(mid, N_EXPERTS - N_GROUPS, 1)
             + pltpu.roll(lo, 2 * N_EXPERTS - N_GROUPS, 1))
    route = jnp.where(lane == ROUTE_MEMBER + g_idx, 1.0, route)
    route_ref[...] = route.astype(BF16)


def _tail(mix, x2d, wo_bf, g1, b1, wr, br_pad, tm, alpha):
    n = x2d.shape[0]
    row = lambda i: (i, 0)
    const = lambda i: (0, 0)
    return pl.pallas_call(
        functools.partial(_tail_kernel, alpha=alpha, split=wr.dtype == BF16),
        grid=(n // tm,),
        in_specs=[pl.BlockSpec((tm, D_MODEL), row), pl.BlockSpec((tm, D_MODEL), row),
                  pl.BlockSpec((D_MODEL, D_MODEL), const),
                  pl.BlockSpec((1, D_MODEL), const), pl.BlockSpec((1, D_MODEL), const),
                  pl.BlockSpec(wr.shape, const), pl.BlockSpec((1, LANES), const)],
        out_specs=[pl.BlockSpec((tm, D_MODEL), row), pl.BlockSpec((tm, D_MODEL), row),
                   pl.BlockSpec((tm, LANES), row), pl.BlockSpec((tm, LANES), row)],
        out_shape=[jax.ShapeDtypeStruct((n, D_MODEL), F32), jax.ShapeDtypeStruct((n, D_MODEL), BF16),
                   jax.ShapeDtypeStruct((n, LANES), F32), jax.ShapeDtypeStruct((n, LANES), BF16)],
        compiler_params=_cparams(("arbitrary",)),
        name="tail",
    )(mix, x2d, wo_bf, g1, b1, wr, br_pad)


def _swiglu_rows(x, gate, wg_ref, wu_ref, wd_ref):
    a = _dot(x, wg_ref[0])
    u = _dot(x, wu_ref[0])
    hmid = (a * jax.nn.sigmoid(a)) * u * gate
    return _dot(hmid.astype(BF16), wd_ref[0])


def _expert_specs(ix):
    return [pl.BlockSpec((1, D_MODEL, D_EXPERT), ix), pl.BlockSpec((1, D_MODEL, D_EXPERT), ix),
            pl.BlockSpec((1, D_EXPERT, D_MODEL), ix)]


def _moe_kernel(hb_ref, h_ref, gate_ref, wg_ref, wu_ref, wd_ref, g_ref, b_ref, y_ref, acc_ref, *, alpha):
    e = pl.program_id(1)

    @pl.when(e == 0)
    def _():
        acc_ref[...] = jnp.zeros(acc_ref.shape, F32)

    xb = hb_ref[...]
    lane = lax.broadcasted_iota(jnp.int32, (xb.shape[0], LANES), 1)
    gate = jnp.sum(jnp.where(lane == e + N_GROUPS, gate_ref[...], 0.0), axis=1, keepdims=True)
    acc_ref[...] += _swiglu_rows(xb, gate, wg_ref, wu_ref, wd_ref)

    @pl.when(e == N_EXPERTS - 1)
    def _():
        y_ref[...] = _layer_norm(alpha * h_ref[...] + acc_ref[...], g_ref[...], b_ref[...])


def _moe(hb, h, gates, wg_bf, wu_bf, wd_bf, g2, b2, tm, alpha):
    n = h.shape[0]
    row = lambda i, e: (i, 0)
    const = lambda i, e: (0, 0)
    return pl.pallas_call(
        functools.partial(_moe_kernel, alpha=alpha),
        grid=(n // tm, N_EXPERTS),
        in_specs=[pl.BlockSpec((tm, D_MODEL), row), pl.BlockSpec((tm, D_MODEL), row),
                  pl.BlockSpec((tm, LANES), row)] + _expert_specs(lambda i, e: (e, 0, 0))
                 + [pl.BlockSpec((1, D_MODEL), const), pl.BlockSpec((1, D_MODEL), const)],
        out_specs=pl.BlockSpec((tm, D_MODEL), row),
        out_shape=jax.ShapeDtypeStruct((n, D_MODEL), F32),
        scratch_shapes=[pltpu.VMEM((tm, D_MODEL), F32)],
        compiler_params=_cparams(("arbitrary", "arbitrary")),
        name="moe",
    )(hb, h, gates, wg_bf, wu_bf, wd_bf, g2, b2)


def _route_gate(route, lane, e):
    pick = (lane == e) | (lane == e + N_EXPERTS) | (lane == e + 2 * N_EXPERTS)
    return jnp.sum(jnp.where(pick, route, 0.0), axis=1, keepdims=True)


def _moe_grouped_kernel(hb_ref, h_ref, route_ref, wg_ref, wu_ref, wd_ref, g_ref, b_ref, y_ref,
                        acc_ref, rcol_ref, rrow_ref, mrow_ref, xg_ref, gsel_ref, yg_ref, cnt_ref,
                        *, alpha, cap):
    e = pl.program_id(1)
    grp = e // EXPERTS_PER_GROUP
    ts = hb_ref.shape[0]
    cap_pad = yg_ref.shape[0]
    lane = lax.broadcasted_iota(jnp.int32, (ts, LANES), 1)

    @pl.when(e == 0)
    def _():
        acc_ref[...] = jnp.zeros(acc_ref.shape, F32)
        t = lax.broadcasted_iota(jnp.int32, (LANES, LANES), 0)
        u = lax.broadcasted_iota(jnp.int32, (LANES, LANES), 1)
        lower = (u < t).astype(BF16)
        carry_c = jnp.zeros((1, LANES), F32)
        carry_r = jnp.zeros((GROUP_ROWS, 1), F32)
        for blk in range(ts // LANES):
            sl = slice(blk * LANES, (blk + 1) * LANES)
            memb = jnp.where((u >= ROUTE_MEMBER) & (u < ROUTE_MEMBER + N_GROUPS),
                             route_ref[sl, :].astype(F32), 0.0)
            rcol_ref[sl, :] = _dot(lower, memb.astype(BF16)) + carry_c
            carry_c = carry_c + jnp.sum(memb, axis=0, keepdims=True)
            mrow = memb.T[ROUTE_MEMBER:ROUTE_MEMBER + GROUP_ROWS, :]
            mrow_ref[:, sl] = mrow
            rrow_ref[:, sl] = _dot_nt(mrow.astype(BF16), lower) + carry_r
            carry_r = carry_r + jnp.sum(mrow, axis=1, keepdims=True)
        lane1 = lax.broadcasted_iota(jnp.int32, (1, LANES), 1)
        for gi in range(N_GROUPS):
            cnt_ref[gi] = jnp.sum(jnp.where(lane1 == ROUTE_MEMBER + gi, carry_c, 0.0)).astype(jnp.int32)

    small = cnt_ref[grp] <= cap

    @pl.when((e % EXPERTS_PER_GROUP == 0) & small)
    def _():
        pos = lax.broadcasted_iota(jnp.int32, (cap, ts), 0).astype(F32)
        rank = jnp.where(mrow_ref[pl.ds(grp, 1), :] > 0.5, rrow_ref[pl.ds(grp, 1), :], -1.0)
        onehot = jnp.where(rank == pos, 1.0, 0.0).astype(BF16)
        xg_ref[...] = _dot(onehot, hb_ref[...]).astype(BF16)
        gsel_ref[...] = _dot(onehot, route_ref[...])
        yg_ref[...] = jnp.zeros(yg_ref.shape, F32)

    @pl.when(small)
    def _():
        lane_c = lax.broadcasted_iota(jnp.int32, (cap, LANES), 1)
        gate = _route_gate(gsel_ref[...], lane_c, e)
        yg_ref[0:cap, :] += _swiglu_rows(xg_ref[...], gate, wg_ref, wu_ref, wd_ref)

    @pl.when(jnp.logical_not(small))
    def _():
        gate = _route_gate(route_ref[...].astype(F32), lane, e)
        acc_ref[...] += _swiglu_rows(hb_ref[...], gate, wg_ref, wu_ref, wd_ref)

    @pl.when((e % EXPERTS_PER_GROUP == EXPERTS_PER_GROUP - 1) & small)
    def _():
        mine = lane == ROUTE_MEMBER + grp
        rank = jnp.sum(jnp.where(mine, rcol_ref[...], 0.0), axis=1, keepdims=True)
        member = jnp.sum(jnp.where(mine, route_ref[...].astype(F32), 0.0), axis=1, keepdims=True)
        pos = lax.broadcasted_iota(jnp.int32, (ts, cap_pad), 1).astype(F32)
        rank = jnp.where(member > 0.5, rank, -1.0)
        onehot_t = jnp.where(rank == pos, 1.0, 0.0).astype(BF16)
        acc_ref[...] += _dot(onehot_t, yg_ref[...].astype(BF16))

    @pl.when(e == N_EXPERTS - 1)
    def _():
        y_ref[...] = _layer_norm(alpha * h_ref[...] + acc_ref[...], g_ref[...], b_ref[...])


def _moe_grouped(hb, h, route, wg_bf, wu_bf, wd_bf, g2, b2, ts, cap, alpha):
    n = h.shape[0]
    cap_pad = -(-cap // LANES) * LANES
    row = lambda i, e: (i, 0)
    const = lambda i, e: (0, 0)
    return pl.pallas_call(
        functools.partial(_moe_grouped_kernel, alpha=alpha, cap=cap),
        grid=(n // ts, N_EXPERTS),
        in_specs=[pl.BlockSpec((ts, D_MODEL), row), pl.BlockSpec((ts, D_MODEL), row),
                  pl.BlockSpec((ts, LANES), row)] + _expert_specs(lambda i, e: (e, 0, 0))
                 + [pl.BlockSpec((1, D_MODEL), const), pl.BlockSpec((1, D_MODEL), const)],
        out_specs=pl.BlockSpec((ts, D_MODEL), row),
        out_shape=jax.ShapeDtypeStruct((n, D_MODEL), F32),
        scratch_shapes=[pltpu.VMEM((ts, D_MODEL), F32),
                        pltpu.VMEM((ts, LANES), F32),
                        pltpu.VMEM((GROUP_ROWS, ts), F32),
                        pltpu.VMEM((GROUP_ROWS, ts), F32),
                        pltpu.VMEM((cap, D_MODEL), BF16),
                        pltpu.VMEM((cap, LANES), F32),
                        pltpu.VMEM((cap_pad, D_MODEL), F32),
                        pltpu.SMEM((N_GROUPS,), jnp.int32)],
        compiler_params=_cparams(("arbitrary", "arbitrary")),
        name="moe_grouped",
    )(hb, h, route, wg_bf, wu_bf, wd_bf, g2, b2)


def _rope_tables(pos):
    half = HEAD_DIM // 2
    inv = ROPE_THETA ** (-jnp.arange(half, dtype=F32) * 2.0 / HEAD_DIM)
    ang = pos.astype(F32)[:, None] * inv[None, :]
    cos, sin = jnp.cos(ang), jnp.sin(ang)
    reps = WIDTH // HEAD_DIM
    cos_t = jnp.tile(jnp.concatenate([cos, cos], axis=1), (1, reps))
    sin_t = jnp.tile(jnp.concatenate([-sin, sin], axis=1), (1, reps))
    return cos_t, sin_t


def kernel(x_prompt, x_sample, cache_diff_k, cache_diff_v, cache_fox_k, cache_fox_v, cache_fox_logf, page_table,
           w_in, b_forget, lambda_q1, lambda_k1, lambda_q2, lambda_k2, subln_g, w_out, ln1_g, ln1_b,
           w_router_group, b_router_group, w_router_expert, b_router_expert, w_gate, w_up, w_down, ln2_g, ln2_b):
    B, S, D = x_prompt.shape
    DB, T, _ = x_sample.shape
    depth = w_in.shape[0]
    assert depth == 1 and T == 1 and D == D_MODEL
    n_pool, page = cache_diff_k.shape[1], cache_diff_k.shape[2]
    n_pages = page_table.shape[1]
    past_len = n_pages * page
    assert page == LANES
    alpha = (2 * depth) ** 0.25
    l = 0
    lam_init = 0.8 - 0.6 * math.exp(-0.3 * l)

    w_pad = jnp.pad(w_in[l], ((0, 0), (0, 6 * WIDTH + LANES - w_in.shape[2]))).astype(BF16)
    bf_pad = jnp.pad(b_forget[l], (0, LANES - FOX_HEADS)).reshape(1, LANES)
    lamv = jnp.stack([lambda_q1[l], lambda_k1[l], lambda_q2[l], lambda_k2[l]])
    g1 = subln_g[l].reshape(1, LANES)
    wo_bf = w_out[l].astype(BF16)
    wr_pad = jnp.pad(jnp.concatenate([w_router_group[l], w_router_expert[l]], axis=1),
                     ((0, 0), (0, LANES - N_GROUPS - N_EXPERTS)))
    br_pad = jnp.pad(jnp.concatenate([b_router_group[l], b_router_expert[l]]),
                     (0, LANES - N_GROUPS - N_EXPERTS)).reshape(1, LANES)
    wg_bf, wu_bf, wd_bf = w_gate[l].astype(BF16), w_up[l].astype(BF16), w_down[l].astype(BF16)
    ln1g, ln1b = ln1_g[l].reshape(1, D), ln1_b[l].reshape(1, D)
    ln2g, ln2b = ln2_g[l].reshape(1, D), ln2_b[l].reshape(1, D)
    p_cos, p_sin = _rope_tables(jnp.arange(S, dtype=jnp.int32))
    s_cos, s_sin = _rope_tables(jnp.full((DB,), past_len, dtype=jnp.int32))

    tq = ATTN_TILE
    xp2 = x_prompt.reshape(B * S, D)
    (dq, dkb, fq, fkb, dk, dv, fk, fv, logf, dvt, fvt) = _project(xp2, w_pad, p_cos, p_sin, bf_pad, tq, S // tq, True)
    logf_t = jnp.swapaxes(logf.reshape(B, S, FOX_HEADS), 1, 2).reshape(B * FOX_HEADS, S)
    c_t = _cumsum_rows(logf_t).reshape(B, FOX_HEADS, S)
    c = jnp.swapaxes(c_t, 1, 2)
    c_blk = jnp.swapaxes(c_t.reshape(B, FOX_HEADS, S // tq, tq), 1, 2)
    r3 = lambda a: a.reshape(B, S, WIDTH)
    vt4 = lambda a: a.reshape(B, S // tq, WIDTH, tq)
    mix_p = _prompt_attention(r3(dq), r3(fq), r3(dkb), r3(fkb), vt4(dvt), vt4(fvt), c, c_blk, lamv, g1,
                              tq, lam_init)
    wr_hi = wr_pad.astype(BF16)
    wr_split = jnp.concatenate([wr_hi, (wr_pad - wr_hi.astype(F32)).astype(BF16)], axis=1)
    h_p, hb_p, _, route_p = _tail(mix_p.reshape(B * S, D), xp2, wo_bf, ln1g, ln1b, wr_split, br_pad, 256, alpha)
    y_p = _moe_grouped(hb_p, h_p, route_p, wg_bf, wu_bf, wd_bf, ln2g, ln2b, MOE_TILE, MOE_GROUP_CAP, alpha)

    xs2 = x_sample.reshape(DB, D)
    (sdq, _, sfq, _, sdk, sdv, sfk, sfv, slogf) = _project(xs2, w_pad, s_cos, s_sin, bf_pad, DB, 1, False)
    diff_pool = lambda a: a[l].reshape(n_pool, page * DIFF_HEADS, 2 * HEAD_DIM)
    fox_pool = lambda a: jnp.transpose(a[l], (0, 2, 3, 1)).reshape(n_pool, FOX_HEADS * HEAD_DIM, page)
    lft = jnp.transpose(cache_fox_logf[l], (0, 2, 1)).reshape(n_pool * FOX_HEADS, page)
    ex, tot = _page_suffix(lft, 4096)
    rep = lambda a: jnp.repeat(a.astype(F32).reshape(DB, DIFF_HEADS, 2 * HEAD_DIM), 2, axis=1)
    v3 = lambda a: a.astype(F32).reshape(DB, 1, WIDTH)
    dec_args = [rep(sdq), v3(sfq), rep(sdk), rep(sdv), v3(sfk), v3(sfv), slogf.reshape(DB, FOX_HEADS, 1), lamv, g1,
                diff_pool(cache_diff_k), diff_pool(cache_diff_v), fox_pool(cache_fox_k), fox_pool(cache_fox_v),
                ex.reshape(n_pool, FOX_HEADS, page), tot.reshape(n_pool, FOX_HEADS, page)]
    od_s, of_s = _decode_attention(page_table, dec_args, DECODE_PAGES, lam_init)
    mix_s = jnp.concatenate([od_s[:, 0::2, :].reshape(DB, WIDTH), of_s.reshape(DB, WIDTH)], axis=1)
    h_s, hb_s, gates_s, _ = _tail(mix_s, xs2, wo_bf, ln1g, ln1b, wr_pad, br_pad, DB, alpha)
    y_s = _moe(hb_s, h_s, gates_s, wg_bf, wu_bf, wd_bf, ln2g, ln2b, DB, alpha)

    fox_out = lambda a: jnp.transpose(a.reshape(B, FOX_HEADS, HEAD_DIM, S), (0, 3, 1, 2))[None]
    return (y_p.reshape(B, S, D), y_s.reshape(DB, T, D),
            dk.reshape(1, B, S, DIFF_HEADS, 2 * HEAD_DIM), dv.reshape(1, B, S, DIFF_HEADS, 2 * HEAD_DIM),
            fox_out(fk), fox_out(fv),
            logf.reshape(1, B, S, FOX_HEADS),
            sdk.reshape(1, DB, T, DIFF_HEADS, 2 * HEAD_DIM), sdv.reshape(1, DB, T, DIFF_HEADS, 2 * HEAD_DIM),
            sfk.reshape(1, DB, T, FOX_HEADS, HEAD_DIM), sfv.reshape(1, DB, T, FOX_HEADS, HEAD_DIM),
            slogf.reshape(1, DB, T, FOX_HEADS))
```

```python
import functools
import math

import jax
import jax.numpy as jnp
from jax import lax
from jax.experimental import pallas as pl
from jax.experimental.pallas import tpu as pltpu

F32 = jnp.float32
BF16 = jnp.bfloat16

HEAD_DIM = 64
DIFF_HEADS = 4
FOX_HEADS = 8
WIDTH = 512
D_MODEL = 1024
N_GROUPS = 4
EXPERTS_PER_GROUP = 4
N_EXPERTS = 16
D_EXPERT = 512
ROPE_THETA = 10000.0
LN_EPS = 1e-5
SUBLN_EPS = 1e-5
LANES = 128
NEG = -1e30
LOG2E = 1.4426950408889634
V7X_VMEM_LIMIT = 48 * 1024 * 1024
ATTN_TILE = 1024
PROJ_TILE = 512
DECODE_PAGES = 16
TAIL_TILE = 512
MOE_TILE = 1024
MOE_GROUP_CAP = 320


def _cparams(sem):
    return pltpu.CompilerParams(dimension_semantics=sem, vmem_limit_bytes=V7X_VMEM_LIMIT)


def _dot(a, b):
    return jnp.dot(a, b, preferred_element_type=F32)


def _dot_nt(a, b):
    return lax.dot_general(a, b, (((1,), (1,)), ((), ())), preferred_element_type=F32)


def _dot_hi(a, b):
    return jnp.dot(a, b, preferred_element_type=F32, precision=lax.Precision.HIGHEST)


def _layer_norm(x, g, b):
    mu = jnp.mean(x, axis=-1, keepdims=True)
    xc = x - mu
    var = jnp.mean(xc * xc, axis=-1, keepdims=True)
    return xc * lax.rsqrt(var + LN_EPS) * g + b


def _diff_lambda(lv, lam_init):
    return (jnp.exp(jnp.sum(lv[0:1] * lv[1:2], axis=1, keepdims=True))
            - jnp.exp(jnp.sum(lv[2:3] * lv[3:4], axis=1, keepdims=True)) + lam_init)


def _proj_kernel(x_ref, w_ref, cos_ref, sin_ref, bf_ref,
                 q_ref, kb_ref, dk_ref, dv_ref, fk_ref, fv_ref, logf_ref, *vt_refs):
    tm = x_ref.shape[0]
    xb = x_ref[...].astype(BF16)
    cos = cos_ref[...]
    sin = sin_ref[...]
    lane = lax.broadcasted_iota(jnp.int32, (tm, WIDTH), 1)
    first_half = (lane & (HEAD_DIM // 2)) == 0
    qscale = HEAD_DIM ** -0.5 * LOG2E

    def mm(c):
        return _dot(xb, w_ref[:, c * WIDTH:(c + 1) * WIDTH])

    def rope(p):
        partner = jnp.where(first_half,
                            pltpu.roll(p, WIDTH - HEAD_DIM // 2, 1),
                            pltpu.roll(p, HEAD_DIM // 2, 1))
        return p * cos + partner * sin

    dq = rope(mm(0))
    q_ref[:, :WIDTH] = (dq * qscale).astype(BF16)
    dk = rope(mm(1))
    for hh in range(DIFF_HEADS):
        dk_ref[:, hh, :] = dk[:, hh * LANES:(hh + 1) * LANES]
    kb_ref[:, :WIDTH] = dk.astype(BF16)
    dv = mm(2)
    for hh in range(DIFF_HEADS):
        dv_ref[:, hh, :] = dv[:, hh * LANES:(hh + 1) * LANES]
    q_ref[:, WIDTH:] = (mm(3) * qscale).astype(BF16)
    fk = mm(4)
    kb_ref[:, WIDTH:] = fk.astype(BF16)
    fv = mm(5)
    if vt_refs:
        fvt = fv.T
        fk_ref[0] = fk.T
        fv_ref[0] = fvt
        vt_refs[0][0, :WIDTH, :] = dv.T.astype(BF16)
        vt_refs[0][0, WIDTH:, :] = fvt.astype(BF16)
    else:
        fk_ref[...] = fk
        fv_ref[...] = fv
    z = _dot(xb, w_ref[:, 6 * WIDTH:6 * WIDTH + LANES]) + bf_ref[...]
    logf = jnp.minimum(z, 0.0) - jnp.log1p(jnp.exp(-jnp.abs(z)))
    logf_ref[...] = logf[:, :FOX_HEADS]


def _project(x2d, w_bf, cos_t, sin_t, bf_pad, tm, table_blocks, prompt_layout):
    n = x2d.shape[0]
    grid = (n // tm,)
    row = lambda i: (i, 0)
    tab = lambda i: (i % table_blocks, 0)
    const = lambda i: (0, 0)
    wide = pl.BlockSpec((tm, WIDTH), row)
    both = pl.BlockSpec((tm, 2 * WIDTH), row)
    out_shape = ([jax.ShapeDtypeStruct((n, 2 * WIDTH), BF16)] * 2
                 + [jax.ShapeDtypeStruct((n, DIFF_HEADS, LANES), F32)] * 2
                 + [jax.ShapeDtypeStruct((n, WIDTH), F32)] * 2
                 + [jax.ShapeDtypeStruct((n, FOX_HEADS), F32)])
    head3 = pl.BlockSpec((tm, DIFF_HEADS, LANES), lambda i: (i, 0, 0))
    out_specs = [both, both, head3, head3, wide, wide, pl.BlockSpec((tm, FOX_HEADS), row)]
    if prompt_layout:
        n_seq = n // (table_blocks * tm)
        by_seq = pl.BlockSpec((1, WIDTH, tm), lambda i: (i // table_blocks, 0, i % table_blocks))
        out_shape[4:6] = [jax.ShapeDtypeStruct((n_seq, WIDTH, table_blocks * tm), F32)] * 2
        out_specs[4:6] = [by_seq, by_seq]
        out_shape += [jax.ShapeDtypeStruct((n // tm, 2 * WIDTH, tm), BF16)]
        out_specs += [pl.BlockSpec((1, 2 * WIDTH, tm), lambda i: (i, 0, 0))]
    return pl.pallas_call(
        _proj_kernel,
        grid=grid,
        in_specs=[pl.BlockSpec((tm, D_MODEL), row),
                  pl.BlockSpec(w_bf.shape, const),
                  pl.BlockSpec((tm, WIDTH), tab),
                  pl.BlockSpec((tm, WIDTH), tab),
                  pl.BlockSpec((1, LANES), const)],
        out_specs=out_specs,
        out_shape=out_shape,
        compiler_params=_cparams(("arbitrary",)),
        name="proj",
    )(x2d, w_bf, cos_t, sin_t, bf_pad)


def _cumsum_kernel(x_ref, o_ref):
    rows, s = x_ref.shape
    t = lax.broadcasted_iota(jnp.int32, (LANES, LANES), 0)
    u = lax.broadcasted_iota(jnp.int32, (LANES, LANES), 1)
    tri = (t <= u).astype(F32)
    carry = jnp.zeros((rows, 1), F32)
    for ch in range(s // LANES):
        blk = _dot_hi(x_ref[:, ch * LANES:(ch + 1) * LANES], tri) + carry
        o_ref[:, ch * LANES:(ch + 1) * LANES] = blk * LOG2E
        carry = blk[:, LANES - 1:LANES]


def _cumsum_rows(xt):
    return pl.pallas_call(
        _cumsum_kernel,
        out_shape=jax.ShapeDtypeStruct(xt.shape, F32),
        name="cumsum",
    )(xt)


def _attn_kernel(dq_ref, fq_ref, dk_ref, fk_ref, dvt_ref, fvt_ref, c_ref, cq_ref, lamv_ref, g_ref,
                 mix_ref, *, tq, lam_init):
    qi = pl.program_id(1)
    tk = tq
    lane = lax.broadcasted_iota(jnp.int32, (tq, LANES), 1)
    lo = lane < HEAD_DIM
    vb = dvt_ref.shape[3]
    causal = lax.broadcasted_iota(jnp.int32, (tk, tq), 0) <= lax.broadcasted_iota(jnp.int32, (tk, tq), 1)
    top = lax.broadcasted_iota(jnp.int32, (LANES, tq), 0) < HEAD_DIM
    lam = _diff_lambda(lamv_ref[...], lam_init)
    g = g_ref[...]

    for u in range(DIFF_HEADS + FOX_HEADS // 2):
        is_diff = u < DIFF_HEADS
        off = (u if is_diff else u - DIFF_HEADS) * LANES
        k_ref, vt_ref = (dk_ref, dvt_ref) if is_diff else (fk_ref, fvt_ref)
        q_pair = (dq_ref if is_diff else fq_ref)[0, :, off:off + LANES]
        zero = jnp.zeros_like(q_pair)
        qs = (jnp.where(lo, q_pair, zero), jnp.where(lo, zero, q_pair))
        h0 = None if is_diff else 2 * (u - DIFF_HEADS)

        def block(j, carry, masked, off=off, k_ref=k_ref, vt_ref=vt_ref, qs=qs, h0=h0):
            start = pl.multiple_of(j * tk, tk)
            k = k_ref[0, pl.ds(start, tk), off:off + LANES]
            scores = [_dot_nt(k, q) for q in qs]
            out = []
            for idx in range(2):
                m, l, acc = carry[idx]
                st = scores[idx]
                if h0 is None:
                    cq = None
                else:
                    st = st - c_ref[0, pl.ds(start, tk), h0 + idx:h0 + idx + 1]
                    cq = cq_ref[0, 0, h0 + idx:h0 + idx + 1, :]
                if masked:
                    st = jnp.where(causal, st, NEG)
                mt = jnp.max(st, axis=0, keepdims=True)
                m_new = jnp.maximum(m, mt if cq is None else mt + cq)
                alpha = jnp.exp2(m - m_new)
                e = jnp.exp2(st - (m_new if cq is None else m_new - cq))
                l = alpha * l + jnp.sum(e, axis=0, keepdims=True)
                e = e.astype(BF16)
                acc = alpha * acc
                for i in range(tk // vb):
                    acc = acc + _dot(vt_ref[0, j * (tk // vb) + i, off:off + LANES, :], e[i * vb:(i + 1) * vb])
                out.append((m_new, l, acc))
            return tuple(out)

        init = tuple((jnp.full((1, tq), NEG, F32), jnp.zeros((1, tq), F32), jnp.zeros((LANES, tq), F32))
                     for _ in range(2))
        carry = lax.fori_loop(0, qi, lambda j, c: block(j, c, False), init)
        (_, la, acca), (_, lb, accb) = block(qi, carry, True)
        if is_diff:
            o = (acca / la - lam * (accb / lb)).T
            ms = jnp.mean(o * o, axis=1, keepdims=True)
            o = o * lax.rsqrt(ms + SUBLN_EPS) * g * (1.0 - lam_init)
            mix_ref[0, :, off:off + LANES] = o.astype(BF16)
        else:
            o = jnp.where(top, acca / la, accb / lb).T
            mix_ref[0, :, WIDTH + off:WIDTH + off + LANES] = o.astype(BF16)


def _prompt_attention(q, k, vt, c, cq, lamv, g, tq, lam_init):
    b, s, _ = q.shape
    n_vb, vb = vt.shape[1], vt.shape[3]
    assert tq % vb == 0
    qspec = lambda half: pl.BlockSpec((1, tq, WIDTH), lambda bi, qi: (bi, qi, half))
    kspec = lambda half: pl.BlockSpec((1, s, WIDTH), lambda bi, qi: (bi, 0, half))
    vspec = lambda half: pl.BlockSpec((1, n_vb, WIDTH, vb), lambda bi, qi: (bi, 0, half, 0))
    return pl.pallas_call(
        functools.partial(_attn_kernel, tq=tq, lam_init=lam_init),
        grid=(b, s // tq),
        in_specs=[qspec(0), qspec(1), kspec(0), kspec(1), vspec(0), vspec(1),
                  pl.BlockSpec((1, s, FOX_HEADS), lambda bi, qi: (bi, 0, 0)),
                  pl.BlockSpec((1, 1, FOX_HEADS, tq), lambda bi, qi: (bi, qi, 0, 0)),
                  pl.BlockSpec((4, HEAD_DIM), lambda bi, qi: (0, 0)),
                  pl.BlockSpec((1, LANES), lambda bi, qi: (0, 0))],
        out_specs=pl.BlockSpec((1, tq, D_MODEL), lambda bi, qi: (bi, qi, 0)),
        out_shape=jax.ShapeDtypeStruct((b, s, D_MODEL), BF16),
        compiler_params=_cparams(("arbitrary", "arbitrary")),
        name="attn",
    )(q, q, k, k, vt, vt, c, cq, lamv, g)


def _suffix_kernel(x_ref, e_ref, t_ref):
    t = lax.broadcasted_iota(jnp.int32, (LANES, LANES), 0)
    u = lax.broadcasted_iota(jnp.int32, (LANES, LANES), 1)
    x = x_ref[...]
    e_ref[...] = _dot_hi(x, (t > u).astype(F32))
    t_ref[...] = _dot_hi(x, jnp.ones((LANES, LANES), F32))


def _page_suffix(lft, tr):
    n = lft.shape[0]
    spec = pl.BlockSpec((tr, LANES), lambda i: (i, 0))
    return pl.pallas_call(
        _suffix_kernel,
        grid=(n // tr,),
        in_specs=[spec],
        out_specs=[spec, spec],
        out_shape=[jax.ShapeDtypeStruct(lft.shape, F32)] * 2,
        compiler_params=_cparams(("arbitrary",)),
        name="suffix",
    )(lft)


N_DECODE_SMALL = 9


def _decode_parts(j, n_steps, refs, pages, lam_init, valid=None):
    live = (lambda c: c) if valid is None else (lambda c: c & valid)
    keep = (lambda new, old: new) if valid is None else (lambda new, old: jnp.where(valid, new, old))
    qd_ref, qf_ref, knd_ref, vnd_ref, knf_ref, vnf_ref, slogf_ref, lamv_ref, g_ref = refs[:N_DECODE_SMALL]
    rest = refs[N_DECODE_SMALL:]
    kd = rest[0:pages]
    vd = rest[pages:2 * pages]
    kf = rest[2 * pages:3 * pages]
    vf = rest[3 * pages:4 * pages]
    ex = rest[4 * pages:5 * pages]
    tot = rest[5 * pages:6 * pages]
    od_ref, of_ref = rest[6 * pages], rest[6 * pages + 1]
    md, ld, accd, mf, lf, accf, carry = rest[6 * pages + 2:]
    rows = 2 * DIFF_HEADS
    th = DIFF_HEADS * LANES

    def init():
        @pl.when(live(j == 0))
        def _():
            md[...] = jnp.full(md.shape, NEG, F32)
            mf[...] = jnp.full(mf.shape, NEG, F32)
            ld[...] = jnp.zeros(ld.shape, F32)
            lf[...] = jnp.zeros(lf.shape, F32)
            accd[...] = jnp.zeros(accd.shape, F32)
            accf[...] = jnp.zeros(accf.shape, F32)
            carry[...] = jnp.zeros(carry.shape, F32)

    def queries():
        lane1 = lax.broadcasted_iota(jnp.int32, (rows, LANES), 1)
        r1 = lax.broadcasted_iota(jnp.int32, (rows, LANES), 0)
        qd = jnp.where((lane1 >> 6) == (r1 & 1), qd_ref[0], 0.0)
        col = lax.broadcasted_iota(jnp.int32, (rows, th), 1)
        r = lax.broadcasted_iota(jnp.int32, (rows, th), 0)
        own_head = (col & (DIFF_HEADS - 1)) == (r >> 1)
        half_mask = (col >> 6) == r
        qf = jnp.where(half_mask, qf_ref[0], 0.0)
        return qd, qf, own_head, half_mask, r1

    def update(m_ref, l_ref, acc_ref, s_list, pv_fn):
        s = jnp.concatenate(s_list, axis=1)
        w = s.shape[1] // pages
        m_old = m_ref[...]
        m_new = jnp.maximum(m_old, jnp.max(s, axis=1, keepdims=True))
        alpha = jnp.exp2(m_old - m_new)
        e = jnp.exp2(s - m_new)
        l_old = l_ref[...]
        l_ref[...] = keep(alpha * l_old + jnp.sum(e, axis=1, keepdims=True), l_old)
        pv = pv_fn(e[:, 0:w], 0)
        for k in range(1, pages):
            pv = pv + pv_fn(e[:, k * w:(k + 1) * w], k)
        acc_old = acc_ref[...]
        acc_ref[...] = keep(alpha * acc_old + pv, acc_old)
        m_ref[...] = keep(m_new, m_old)

    def scores():
        qd, qf, own_head, _, _ = queries()
        sd = [jnp.where(own_head, _dot_nt(qd, kd[k][...]), NEG) for k in range(pages)]
        run_old = carry[...]
        run = run_old
        base = slogf_ref[0]
        sf = [None] * pages
        for k in reversed(range(pages)):
            sf[k] = _dot(qf, kf[k][...]) + LOG2E * ((base + run) + ex[k][...])
            run = run + tot[k][...]
        carry[...] = keep(run, run_old)
        return sd, sf

    def absorb_diff(sd):
        update(md, ld, accd, sd, lambda e, k: _dot(e, vd[k][...]))

    def absorb_fox(sf):
        update(mf, lf, accf, sf, lambda e, k: _dot_nt(e, vf[k][...]))

    def final():
        @pl.when(live(j == n_steps - 1))
        def _():
            qd, qf, _, half_mask, r1 = queries()
            lam = _diff_lambda(lamv_ref[...], lam_init)

            def finish(m_ref, l_ref, acc_ref, q8, k_new, v_new):
                s_new = jnp.sum(q8 * k_new, axis=1, keepdims=True)
                m_old = m_ref[...]
                m_fin = jnp.maximum(m_old, s_new)
                a = jnp.exp2(m_old - m_fin)
                en = jnp.exp2(s_new - m_fin)
                l_fin = a * l_ref[...] + en
                return (a * acc_ref[...] + en * v_new) / l_fin

            od8 = finish(md, ld, accd, qd, knd_ref[0], vnd_ref[0])
            comb = od8 * jnp.where((r1 & 1) == 0, 1.0, -lam)
            o = comb + pltpu.roll(comb, rows - 1, 0)
            ms = jnp.mean(o * o, axis=1, keepdims=True)
            od_ref[0] = o * lax.rsqrt(ms + SUBLN_EPS) * g_ref[...] * (1.0 - lam_init)
            of8 = finish(mf, lf, accf, qf, knf_ref[0], vnf_ref[0])
            of_ref[0] = jnp.sum(jnp.where(half_mask, of8, 0.0), axis=0, keepdims=True)

    return init, scores, absorb_diff, absorb_fox, final


def _decode_kernel(pt_ref, *refs, pages, lam_init):
    del pt_ref
    init, scores, absorb_diff, absorb_fox, final = _decode_parts(pl.program_id(1), pl.num_programs(1), refs,
                                                                 pages, lam_init)
    init()
    sd, sf = scores()
    absorb_diff(sd)
    absorb_fox(sf)
    final()


def _decode_attention(page_table, dec_args, pages, lam_init):
    db, n_pages = page_table.shape
    n_steps = n_pages // pages
    rows = FOX_HEADS

    def page_map(p):
        return lambda b, j, pt: (pt[b, (n_steps - 1 - j) * pages + p], 0, 0)

    per_seq = lambda b, j, pt: (b, 0, 0)
    const = lambda b, j, pt: (0, 0)
    r128 = pl.BlockSpec((1, rows, LANES), per_seq)
    v512 = pl.BlockSpec((1, 1, WIDTH), per_seq)
    in_specs = [r128, v512, r128, r128, v512, v512,
                pl.BlockSpec((1, FOX_HEADS, 1), per_seq),
                pl.BlockSpec((4, HEAD_DIM), const),
                pl.BlockSpec((1, LANES), const)]
    assert len(in_specs) == N_DECODE_SMALL
    for _ in range(4):
        in_specs += [pl.BlockSpec((None, WIDTH, LANES), page_map(p)) for p in range(pages)]
    for _ in range(2):
        in_specs += [pl.BlockSpec((None, FOX_HEADS, LANES), page_map(p)) for p in range(pages)]
    grid_spec = pltpu.PrefetchScalarGridSpec(
        num_scalar_prefetch=1,
        grid=(db, n_steps),
        in_specs=in_specs,
        out_specs=[r128, v512],
        scratch_shapes=[pltpu.VMEM((rows, 1), F32), pltpu.VMEM((rows, 1), F32), pltpu.VMEM((rows, LANES), F32),
                        pltpu.VMEM((rows, 1), F32), pltpu.VMEM((rows, 1), F32), pltpu.VMEM((rows, WIDTH), F32),
                        pltpu.VMEM((rows, LANES), F32)],
    )
    pools = dec_args[N_DECODE_SMALL:]
    args = list(dec_args[:N_DECODE_SMALL]) + [p for pool in pools for p in [pool] * pages]
    return pl.pallas_call(
        functools.partial(_decode_kernel, pages=pages, lam_init=lam_init),
        grid_spec=grid_spec,
        out_shape=[jax.ShapeDtypeStruct((db, rows, LANES), F32), jax.ShapeDtypeStruct((db, 1, WIDTH), F32)],
        compiler_params=_cparams(("arbitrary", "arbitrary")),
        name="decode",
    )(page_table, *args)


ROUTE_MEMBER = 3 * N_EXPERTS
GROUP_ROWS = 16


def _tail_kernel(mix_ref, x_ref, wo_ref, g_ref, b_ref, wr_ref, br_ref, h_ref, hb_ref, gate_ref, route_ref,
                 *, alpha, split):
    y = _dot(mix_ref[...].astype(BF16), wo_ref[...])
    h = _layer_norm(alpha * x_ref[...] + y, g_ref[...], b_ref[...])
    h_ref[...] = h
    hb = h.astype(BF16)
    hb_ref[...] = hb
    if split:
        two = _dot(hb, wr_ref[...])
        h_lo = (h - hb.astype(F32)).astype(BF16)
        logits = (two[:, :LANES] + two[:, LANES:]) + _dot(h_lo, wr_ref[:, :LANES]) + br_ref[...]
    else:
        logits = _dot_hi(h, wr_ref[...]) + br_ref[...]
    tm = logits.shape[0]
    lane = lax.broadcasted_iota(jnp.int32, (tm, LANES), 1).astype(F32)
    big = float(LANES)
    gmask = lane < N_GROUPS
    gl = jnp.where(gmask, logits, NEG)
    gmax = jnp.max(gl, axis=1, keepdims=True)
    gsum = jnp.sum(jnp.where(gmask, jnp.exp(gl - gmax), 0.0), axis=1, keepdims=True)
    g_val = 1.0 / gsum
    g_idx = jnp.min(jnp.where(gmask & (gl == gmax), lane, big), axis=1, keepdims=True)
    first = N_GROUPS + EXPERTS_PER_GROUP * g_idx
    emask = (lane >= first) & (lane < first + EXPERTS_PER_GROUP)
    el = jnp.where(emask, logits, NEG)
    v1 = jnp.max(el, axis=1, keepdims=True)
    i1 = jnp.min(jnp.where(emask & (el == v1), lane, big), axis=1, keepdims=True)
    el2 = jnp.where(lane == i1, NEG, el)
    v2 = jnp.max(el2, axis=1, keepdims=True)
    i2 = jnp.min(jnp.where(emask & (el2 == v2) & (lane != i1), lane, big), axis=1, keepdims=True)
    t = jnp.exp(v2 - v1)
    w1 = g_val / (1.0 + t)
    w2 = g_val * t / (1.0 + t)
    gates = jnp.where(lane == i1, w1, 0.0) + jnp.where(lane == i2, w2, 0.0)
    gate_ref[...] = gates
    hi = gates.astype(BF16).astype(F32)
    mid = (gates - hi).astype(BF16).astype(F32)
    lo = (gates - hi - mid).astype(BF16).astype(F32)
    route = (pltpu.roll(hi, LANES - N_GROUPS, 1) + pltpu.roll(mid, N_EXPERTS - N_GROUPS, 1)
             + pltpu.roll(lo, 2 * N_EXPERTS - N_GROUPS, 1))
    route = jnp.where(lane == ROUTE_MEMBER + g_idx, 1.0, route)
    route_ref[...] = route.astype(BF16)


def _tail(mix, x2d, wo_bf, g1, b1, wr, br_pad, tm, alpha):
    n = x2d.shape[0]
    row = lambda i: (i, 0)
    const = lambda i: (0, 0)
    return pl.pallas_call(
        functools.partial(_tail_kernel, alpha=alpha, split=wr.dtype == BF16),
        grid=(n // tm,),
        in_specs=[pl.BlockSpec((tm, D_MODEL), row), pl.BlockSpec((tm, D_MODEL), row),
                  pl.BlockSpec((D_MODEL, D_MODEL), const),
                  pl.BlockSpec((1, D_MODEL), const), pl.BlockSpec((1, D_MODEL), const),
                  pl.BlockSpec(wr.shape, const), pl.BlockSpec((1, LANES), const)],
        out_specs=[pl.BlockSpec((tm, D_MODEL), row), pl.BlockSpec((tm, D_MODEL), row),
                   pl.BlockSpec((tm, LANES), row), pl.BlockSpec((tm, LANES), row)],
        out_shape=[jax.ShapeDtypeStruct((n, D_MODEL), F32), jax.ShapeDtypeStruct((n, D_MODEL), BF16),
                   jax.ShapeDtypeStruct((n, LANES), F32), jax.ShapeDtypeStruct((n, LANES), BF16)],
        compiler_params=_cparams(("arbitrary",)),
        name="tail",
    )(mix, x2d, wo_bf, g1, b1, wr, br_pad)


def _swiglu_rows(x, gate, wg_ref, wu_ref, wd_ref):
    a = _dot(x, wg_ref[0])
    u = _dot(x, wu_ref[0])
    hmid = (a * jax.nn.sigmoid(a)) * u * gate
    return _dot(hmid.astype(BF16), wd_ref[0])


def _expert_specs(ix):
    return [pl.BlockSpec((1, D_MODEL, D_EXPERT), ix), pl.BlockSpec((1, D_MODEL, D_EXPERT), ix),
            pl.BlockSpec((1, D_EXPERT, D_MODEL), ix)]


def _moe_kernel(hb_ref, h_ref, gate_ref, wg_ref, wu_ref, wd_ref, g_ref, b_ref, y_ref, acc_ref, *, alpha):
    e = pl.program_id(1)

    @pl.when(e == 0)
    def _():
        acc_ref[...] = jnp.zeros(acc_ref.shape, F32)

    xb = hb_ref[...]
    lane = lax.broadcasted_iota(jnp.int32, (xb.shape[0], LANES), 1)
    gate = jnp.sum(jnp.where(lane == e + N_GROUPS, gate_ref[...], 0.0), axis=1, keepdims=True)
    acc_ref[...] += _swiglu_rows(xb, gate, wg_ref, wu_ref, wd_ref)

    @pl.when(e == N_EXPERTS - 1)
    def _():
        y_ref[...] = _layer_norm(alpha * h_ref[...] + acc_ref[...], g_ref[...], b_ref[...])


def _moe(hb, h, gates, wg_bf, wu_bf, wd_bf, g2, b2, tm, alpha):
    n = h.shape[0]
    row = lambda i, e: (i, 0)
    const = lambda i, e: (0, 0)
    return pl.pallas_call(
        functools.partial(_moe_kernel, alpha=alpha),
        grid=(n // tm, N_EXPERTS),
        in_specs=[pl.BlockSpec((tm, D_MODEL), row), pl.BlockSpec((tm, D_MODEL), row),
                  pl.BlockSpec((tm, LANES), row)] + _expert_specs(lambda i, e: (e, 0, 0))
                 + [pl.BlockSpec((1, D_MODEL), const), pl.BlockSpec((1, D_MODEL), const)],
        out_specs=pl.BlockSpec((tm, D_MODEL), row),
        out_shape=jax.ShapeDtypeStruct((n, D_MODEL), F32),
        scratch_shapes=[pltpu.VMEM((tm, D_MODEL), F32)],
        compiler_params=_cparams(("arbitrary", "arbitrary")),
        name="moe",
    )(hb, h, gates, wg_bf, wu_bf, wd_bf, g2, b2)


def _route_gate(route, lane, e):
    pick = (lane == e) | (lane == e + N_EXPERTS) | (lane == e + 2 * N_EXPERTS)
    return jnp.sum(jnp.where(pick, route, 0.0), axis=1, keepdims=True)


def _moe_grouped_kernel(hb_ref, h_ref, route_ref, wg_ref, wu_ref, wd_ref, g_ref, b_ref, y_ref,
                        acc_ref, rcol_ref, rrow_ref, mrow_ref, xg_ref, gsel_ref, yg_ref, cnt_ref,
                        *, alpha, cap):
    e = pl.program_id(1)
    grp = e // EXPERTS_PER_GROUP
    ts = hb_ref.shape[0]
    cap_pad = yg_ref.shape[0]
    lane = lax.broadcasted_iota(jnp.int32, (ts, LANES), 1)

    @pl.when(e == 0)
    def _():
        acc_ref[...] = jnp.zeros(acc_ref.shape, F32)
        t = lax.broadcasted_iota(jnp.int32, (LANES, LANES), 0)
        u = lax.broadcasted_iota(jnp.int32, (LANES, LANES), 1)
        lower = (u < t).astype(BF16)
        carry_c = jnp.zeros((1, LANES), F32)
        carry_r = jnp.zeros((GROUP_ROWS, 1), F32)
        for blk in range(ts // LANES):
            sl = slice(blk * LANES, (blk + 1) * LANES)
            memb = jnp.where((u >= ROUTE_MEMBER) & (u < ROUTE_MEMBER + N_GROUPS),
                             route_ref[sl, :].astype(F32), 0.0)
            rcol_ref[sl, :] = _dot(lower, memb.astype(BF16)) + carry_c
            carry_c = carry_c + jnp.sum(memb, axis=0, keepdims=True)
            mrow = memb.T[ROUTE_MEMBER:ROUTE_MEMBER + GROUP_ROWS, :]
            mrow_ref[:, sl] = mrow
            rrow_ref[:, sl] = _dot_nt(mrow.astype(BF16), lower) + carry_r
            carry_r = carry_r + jnp.sum(mrow, axis=1, keepdims=True)
        lane1 = lax.broadcasted_iota(jnp.int32, (1, LANES), 1)
        for gi in range(N_GROUPS):
            cnt_ref[gi] = jnp.sum(jnp.where(lane1 == ROUTE_MEMBER + gi, carry_c, 0.0)).astype(jnp.int32)

    small = cnt_ref[grp] <= cap

    @pl.when((e % EXPERTS_PER_GROUP == 0) & small)
    def _():
        pos = lax.broadcasted_iota(jnp.int32, (cap, ts), 0).astype(F32)
        rank = jnp.where(mrow_ref[pl.ds(grp, 1), :] > 0.5, rrow_ref[pl.ds(grp, 1), :], -1.0)
        onehot = jnp.where(rank == pos, 1.0, 0.0).astype(BF16)
        xg_ref[...] = _dot(onehot, hb_ref[...]).astype(BF16)
        gsel_ref[...] = _dot(onehot, route_ref[...])
        yg_ref[...] = jnp.zeros(yg_ref.shape, F32)

    @pl.when(small)
    def _():
        lane_c = lax.broadcasted_iota(jnp.int32, (cap, LANES), 1)
        gate = _route_gate(gsel_ref[...], lane_c, e)
        yg_ref[0:cap, :] += _swiglu_rows(xg_ref[...], gate, wg_ref, wu_ref, wd_ref)

    @pl.when(jnp.logical_not(small))
    def _():
        gate = _route_gate(route_ref[...].astype(F32), lane, e)
        acc_ref[...] += _swiglu_rows(hb_ref[...], gate, wg_ref, wu_ref, wd_ref)

    @pl.when((e % EXPERTS_PER_GROUP == EXPERTS_PER_GROUP - 1) & small)
    def _():
        mine = lane == ROUTE_MEMBER + grp
        rank = jnp.sum(jnp.where(mine, rcol_ref[...], 0.0), axis=1, keepdims=True)
        member = jnp.sum(jnp.where(mine, route_ref[...].astype(F32), 0.0), axis=1, keepdims=True)
        pos = lax.broadcasted_iota(jnp.int32, (ts, cap_pad), 1).astype(F32)
        rank = jnp.where(member > 0.5, rank, -1.0)
        onehot_t = jnp.where(rank == pos, 1.0, 0.0).astype(BF16)
        acc_ref[...] += _dot(onehot_t, yg_ref[...].astype(BF16))

    @pl.when(e == N_EXPERTS - 1)
    def _():
        y_ref[...] = _layer_norm(alpha * h_ref[...] + acc_ref[...], g_ref[...], b_ref[...])


def _moe_grouped(hb, h, route, wg_bf, wu_bf, wd_bf, g2, b2, ts, cap, alpha):
    n = h.shape[0]
    cap_pad = -(-cap // LANES) * LANES
    row = lambda i, e: (i, 0)
    const = lambda i, e: (0, 0)
    return pl.pallas_call(
        functools.partial(_moe_grouped_kernel, alpha=alpha, cap=cap),
        grid=(n // ts, N_EXPERTS),
        in_specs=[pl.BlockSpec((ts, D_MODEL), row), pl.BlockSpec((ts, D_MODEL), row),
                  pl.BlockSpec((ts, LANES), row)] + _expert_specs(lambda i, e: (e, 0, 0))
                 + [pl.BlockSpec((1, D_MODEL), const), pl.BlockSpec((1, D_MODEL), const)],
        out_specs=pl.BlockSpec((ts, D_MODEL), row),
        out_shape=jax.ShapeDtypeStruct((n, D_MODEL), F32),
        scratch_shapes=[pltpu.VMEM((ts, D_MODEL), F32),
                        pltpu.VMEM((ts, LANES), F32),
                        pltpu.VMEM((GROUP_ROWS, ts), F32),
                        pltpu.VMEM((GROUP_ROWS, ts), F32),
                        pltpu.VMEM((cap, D_MODEL), BF16),
                        pltpu.VMEM((cap, LANES), F32),
                        pltpu.VMEM((cap_pad, D_MODEL), F32),
                        pltpu.SMEM((N_GROUPS,), jnp.int32)],
        compiler_params=_cparams(("arbitrary", "arbitrary")),
        name="moe_grouped",
    )(hb, h, route, wg_bf, wu_bf, wd_bf, g2, b2)


def _rope_tables(pos):
    half = HEAD_DIM // 2
    inv = ROPE_THETA ** (-jnp.arange(half, dtype=F32) * 2.0 / HEAD_DIM)
    ang = pos.astype(F32)[:, None] * inv[None, :]
    cos, sin = jnp.cos(ang), jnp.sin(ang)
    reps = WIDTH // HEAD_DIM
    cos_t = jnp.tile(jnp.concatenate([cos, cos], axis=1), (1, reps))
    sin_t = jnp.tile(jnp.concatenate([-sin, sin], axis=1), (1, reps))
    return cos_t, sin_t


def kernel(x_prompt, x_sample, cache_diff_k, cache_diff_v, cache_fox_k, cache_fox_v, cache_fox_logf, page_table,
           w_in, b_forget, lambda_q1, lambda_k1, lambda_q2, lambda_k2, subln_g, w_out, ln1_g, ln1_b,
           w_router_group, b_router_group, w_router_expert, b_router_expert, w_gate, w_up, w_down, ln2_g, ln2_b):
    B, S, D = x_prompt.shape
    DB, T, _ = x_sample.shape
    depth = w_in.shape[0]
    assert depth == 1 and T == 1 and D == D_MODEL
    n_pool, page = cache_diff_k.shape[1], cache_diff_k.shape[2]
    n_pages = page_table.shape[1]
    past_len = n_pages * page
    assert page == LANES
    alpha = (2 * depth) ** 0.25
    l = 0
    lam_init = 0.8 - 0.6 * math.exp(-0.3 * l)

    w_pad = jnp.pad(w_in[l], ((0, 0), (0, 6 * WIDTH + LANES - w_in.shape[2]))).astype(BF16)
    bf_pad = jnp.pad(b_forget[l], (0, LANES - FOX_HEADS)).reshape(1, LANES)
    lamv = jnp.stack([lambda_q1[l], lambda_k1[l], lambda_q2[l], lambda_k2[l]])
    g1 = subln_g[l].reshape(1, LANES)
    wo_bf = w_out[l].astype(BF16)
    wr_pad = jnp.pad(jnp.concatenate([w_router_group[l], w_router_expert[l]], axis=1),
                     ((0, 0), (0, LANES - N_GROUPS - N_EXPERTS)))
    br_pad = jnp.pad(jnp.concatenate([b_router_group[l], b_router_expert[l]]),
                     (0, LANES - N_GROUPS - N_EXPERTS)).reshape(1, LANES)
    wg_bf, wu_bf, wd_bf = w_gate[l].astype(BF16), w_up[l].astype(BF16), w_down[l].astype(BF16)
    ln1g, ln1b = ln1_g[l].reshape(1, D), ln1_b[l].reshape(1, D)
    ln2g, ln2b = ln2_g[l].reshape(1, D), ln2_b[l].reshape(1, D)
    p_cos, p_sin = _rope_tables(jnp.arange(S, dtype=jnp.int32))
    s_cos, s_sin = _rope_tables(jnp.full((DB,), past_len, dtype=jnp.int32))

    tq, tp = ATTN_TILE, PROJ_TILE
    xp2 = x_prompt.reshape(B * S, D)
    (q_p, k_p, dk, dv, fk, fv, logf, vt_p) = _project(xp2, w_pad, p_cos, p_sin, bf_pad, tp, S // tp, True)
    logf_t = jnp.swapaxes(logf.reshape(B, S, FOX_HEADS), 1, 2).reshape(B * FOX_HEADS, S)
    c_t = _cumsum_rows(logf_t).reshape(B, FOX_HEADS, S)
    c = jnp.swapaxes(c_t, 1, 2)
    c_blk = jnp.swapaxes(c_t.reshape(B, FOX_HEADS, S // tq, tq), 1, 2)
    mix_p = _prompt_attention(q_p.reshape(B, S, 2 * WIDTH), k_p.reshape(B, S, 2 * WIDTH),
                              vt_p.reshape(B, S // tp, 2 * WIDTH, tp), c, c_blk, lamv, g1, tq, lam_init)
    wr_hi = wr_pad.astype(BF16)
    wr_split = jnp.concatenate([wr_hi, (wr_pad - wr_hi.astype(F32)).astype(BF16)], axis=1)
    h_p, hb_p, _, route_p = _tail(mix_p.reshape(B * S, D), xp2, wo_bf, ln1g, ln1b, wr_split, br_pad, TAIL_TILE, alpha)
    y_p = _moe_grouped(hb_p, h_p, route_p, wg_bf, wu_bf, wd_bf, ln2g, ln2b, MOE_TILE, MOE_GROUP_CAP, alpha)

    xs2 = x_sample.reshape(DB, D)
    (q_s, _, sdk, sdv, sfk, sfv, slogf) = _project(xs2, w_pad, s_cos, s_sin, bf_pad, DB, 1, False)
    sdq, sfq = q_s[:, :WIDTH], q_s[:, WIDTH:]
    diff_pool = lambda a: a[l].reshape(n_pool, page * DIFF_HEADS, 2 * HEAD_DIM)
    fox_pool = lambda a: jnp.transpose(a[l], (0, 2, 3, 1)).reshape(n_pool, FOX_HEADS * HEAD_DIM, page)
    lft = jnp.transpose(cache_fox_logf[l], (0, 2, 1)).reshape(n_pool * FOX_HEADS, page)
    ex, tot = _page_suffix(lft, 4096)
    rep = lambda a: jnp.repeat(a.astype(F32).reshape(DB, DIFF_HEADS, 2 * HEAD_DIM), 2, axis=1)
    v3 = lambda a: a.astype(F32).reshape(DB, 1, WIDTH)
    dec_args = [rep(sdq), v3(sfq), rep(sdk), rep(sdv), v3(sfk), v3(sfv), slogf.reshape(DB, FOX_HEADS, 1), lamv, g1,
                diff_pool(cache_diff_k), diff_pool(cache_diff_v), fox_pool(cache_fox_k), fox_pool(cache_fox_v),
                ex.reshape(n_pool, FOX_HEADS, page), tot.reshape(n_pool, FOX_HEADS, page)]
    od_s, of_s = _decode_attention(page_table, dec_args, DECODE_PAGES, lam_init)
    mix_s = jnp.concatenate([od_s[:, 0::2, :].reshape(DB, WIDTH), of_s.reshape(DB, WIDTH)], axis=1)
    h_s, hb_s, gates_s, _ = _tail(mix_s, xs2, wo_bf, ln1g, ln1b, wr_pad, br_pad, DB, alpha)
    y_s = _moe(hb_s, h_s, gates_s, wg_bf, wu_bf, wd_bf, ln2g, ln2b, DB, alpha)

    fox_out = lambda a: jnp.transpose(a.reshape(B, FOX_HEADS, HEAD_DIM, S), (0, 3, 1, 2))[None]
    return (y_p.reshape(B, S, D), y_s.reshape(DB, T, D),
            dk.reshape(1, B, S, DIFF_HEADS, 2 * HEAD_DIM), dv.reshape(1, B, S, DIFF_HEADS, 2 * HEAD_DIM),
            fox_out(fk), fox_out(fv),
            logf.reshape(1, B, S, FOX_HEADS),
            sdk.reshape(1, DB, T, DIFF_HEADS, 2 * HEAD_DIM), sdv.reshape(1, DB, T, DIFF_HEADS, 2 * HEAD_DIM),
            sfk.reshape(1, DB, T, FOX_HEADS, HEAD_DIM), sfv.reshape(1, DB, T, FOX_HEADS, HEAD_DIM),
            slogf.reshape(1, DB, T, FOX_HEADS))
```

```python
import functools
import math

import jax
import jax.numpy as jnp
from jax import lax
from jax.experimental import pallas as pl
from jax.experimental.pallas import tpu as pltpu

F32 = jnp.float32
BF16 = jnp.bfloat16

HEAD_DIM = 64
DIFF_HEADS = 4
FOX_HEADS = 8
WIDTH = 512
D_MODEL = 1024
N_GROUPS = 4
EXPERTS_PER_GROUP = 4
N_EXPERTS = 16
D_EXPERT = 512
ROPE_THETA = 10000.0
LN_EPS = 1e-5
SUBLN_EPS = 1e-5
LANES = 128
NEG = -1e30
LOG2E = 1.4426950408889634
V7X_VMEM_LIMIT = 48 * 1024 * 1024
ATTN_TILE = 512
PROJ_TILE = 512
DECODE_PAGES = 16
TAIL_TILE = 512
MOE_TILE = 1024
MOE_GROUP_CAP = 320


def _cparams(sem):
    return pltpu.CompilerParams(dimension_semantics=sem, vmem_limit_bytes=V7X_VMEM_LIMIT)


def _dot(a, b):
    return jnp.dot(a, b, preferred_element_type=F32)


def _dot_nt(a, b):
    return lax.dot_general(a, b, (((1,), (1,)), ((), ())), preferred_element_type=F32)


def _dot_hi(a, b):
    return jnp.dot(a, b, preferred_element_type=F32, precision=lax.Precision.HIGHEST)


def _layer_norm(x, g, b):
    mu = jnp.mean(x, axis=-1, keepdims=True)
    xc = x - mu
    var = jnp.mean(xc * xc, axis=-1, keepdims=True)
    return xc * lax.rsqrt(var + LN_EPS) * g + b


def _diff_lambda(lv, lam_init):
    return (jnp.exp(jnp.sum(lv[0:1] * lv[1:2], axis=1, keepdims=True))
            - jnp.exp(jnp.sum(lv[2:3] * lv[3:4], axis=1, keepdims=True)) + lam_init)


def _proj_kernel(x_ref, w_ref, cos_ref, sin_ref, bf_ref,
                 q_ref, kb_ref, dk_ref, dv_ref, fk_ref, fv_ref, logf_ref, *vt_refs):
    tm = x_ref.shape[0]
    xb = x_ref[...].astype(BF16)
    cos = cos_ref[...]
    sin = sin_ref[...]
    lane = lax.broadcasted_iota(jnp.int32, (tm, WIDTH), 1)
    first_half = (lane & (HEAD_DIM // 2)) == 0
    qscale = HEAD_DIM ** -0.5 * LOG2E

    def mm(c):
        return _dot(xb, w_ref[:, c * WIDTH:(c + 1) * WIDTH])

    def rope(p):
        partner = jnp.where(first_half,
                            pltpu.roll(p, WIDTH - HEAD_DIM // 2, 1),
                            pltpu.roll(p, HEAD_DIM // 2, 1))
        return p * cos + partner * sin

    dq = rope(mm(0))
    q_ref[:, :WIDTH] = (dq * qscale).astype(BF16)
    dk = rope(mm(1))
    for hh in range(DIFF_HEADS):
        dk_ref[:, hh, :] = dk[:, hh * LANES:(hh + 1) * LANES]
    kb_ref[:, :WIDTH] = dk.astype(BF16)
    dv = mm(2)
    for hh in range(DIFF_HEADS):
        dv_ref[:, hh, :] = dv[:, hh * LANES:(hh + 1) * LANES]
    q_ref[:, WIDTH:] = (mm(3) * qscale).astype(BF16)
    fk = mm(4)
    kb_ref[:, WIDTH:] = fk.astype(BF16)
    fv = mm(5)
    if vt_refs:
        fvt = fv.T
        fk_ref[0] = fk.T
        fv_ref[0] = fvt
        vt_refs[0][0, :WIDTH, :] = dv.T.astype(BF16)
        vt_refs[0][0, WIDTH:, :] = fvt.astype(BF16)
    else:
        fk_ref[...] = fk
        fv_ref[...] = fv
    z = _dot(xb, w_ref[:, 6 * WIDTH:6 * WIDTH + LANES]) + bf_ref[...]
    logf = jnp.minimum(z, 0.0) - jnp.log1p(jnp.exp(-jnp.abs(z)))
    logf_ref[...] = logf[:, :FOX_HEADS]


def _project(x2d, w_bf, cos_t, sin_t, bf_pad, tm, table_blocks, prompt_layout):
    n = x2d.shape[0]
    grid = (n // tm,)
    row = lambda i: (i, 0)
    tab = lambda i: (i % table_blocks, 0)
    const = lambda i: (0, 0)
    wide = pl.BlockSpec((tm, WIDTH), row)
    both = pl.BlockSpec((tm, 2 * WIDTH), row)
    out_shape = ([jax.ShapeDtypeStruct((n, 2 * WIDTH), BF16)] * 2
                 + [jax.ShapeDtypeStruct((n, DIFF_HEADS, LANES), F32)] * 2
                 + [jax.ShapeDtypeStruct((n, WIDTH), F32)] * 2
                 + [jax.ShapeDtypeStruct((n, FOX_HEADS), F32)])
    head3 = pl.BlockSpec((tm, DIFF_HEADS, LANES), lambda i: (i, 0, 0))
    out_specs = [both, both, head3, head3, wide, wide, pl.BlockSpec((tm, FOX_HEADS), row)]
    if prompt_layout:
        n_seq = n // (table_blocks * tm)
        by_seq = pl.BlockSpec((1, WIDTH, tm), lambda i: (i // table_blocks, 0, i % table_blocks))
        out_shape[4:6] = [jax.ShapeDtypeStruct((n_seq, WIDTH, table_blocks * tm), F32)] * 2
        out_specs[4:6] = [by_seq, by_seq]
        out_shape += [jax.ShapeDtypeStruct((n // tm, 2 * WIDTH, tm), BF16)]
        out_specs += [pl.BlockSpec((1, 2 * WIDTH, tm), lambda i: (i, 0, 0))]
    return pl.pallas_call(
        _proj_kernel,
        grid=grid,
        in_specs=[pl.BlockSpec((tm, D_MODEL), row),
                  pl.BlockSpec(w_bf.shape, const),
                  pl.BlockSpec((tm, WIDTH), tab),
                  pl.BlockSpec((tm, WIDTH), tab),
                  pl.BlockSpec((1, LANES), const)],
        out_specs=out_specs,
        out_shape=out_shape,
        compiler_params=_cparams(("arbitrary",)),
        name="proj",
    )(x2d, w_bf, cos_t, sin_t, bf_pad)


def _cumsum_kernel(x_ref, o_ref):
    rows, s = x_ref.shape
    t = lax.broadcasted_iota(jnp.int32, (LANES, LANES), 0)
    u = lax.broadcasted_iota(jnp.int32, (LANES, LANES), 1)
    tri = (t <= u).astype(F32)
    carry = jnp.zeros((rows, 1), F32)
    for ch in range(s // LANES):
        blk = _dot_hi(x_ref[:, ch * LANES:(ch + 1) * LANES], tri) + carry
        o_ref[:, ch * LANES:(ch + 1) * LANES] = blk * LOG2E
        carry = blk[:, LANES - 1:LANES]


def _cumsum_rows(xt):
    return pl.pallas_call(
        _cumsum_kernel,
        out_shape=jax.ShapeDtypeStruct(xt.shape, F32),
        name="cumsum",
    )(xt)


def _attn_kernel(dq_ref, fq_ref, dk_ref, fk_ref, dvt_ref, fvt_ref, c_ref, cq_ref, lamv_ref, g_ref,
                 mix_ref, *, tq, lam_init):
    qi = pl.program_id(1)
    tk = tq
    lane = lax.broadcasted_iota(jnp.int32, (tq, LANES), 1)
    lo = lane < HEAD_DIM
    vb = dvt_ref.shape[3]
    causal = lax.broadcasted_iota(jnp.int32, (tk, tq), 0) <= lax.broadcasted_iota(jnp.int32, (tk, tq), 1)
    top = lax.broadcasted_iota(jnp.int32, (LANES, tq), 0) < HEAD_DIM
    lam = _diff_lambda(lamv_ref[...], lam_init)
    g = g_ref[...]

    for u in range(DIFF_HEADS + FOX_HEADS // 2):
        is_diff = u < DIFF_HEADS
        off = (u if is_diff else u - DIFF_HEADS) * LANES
        k_ref, vt_ref = (dk_ref, dvt_ref) if is_diff else (fk_ref, fvt_ref)
        q_pair = (dq_ref if is_diff else fq_ref)[0, :, off:off + LANES]
        zero = jnp.zeros_like(q_pair)
        qs = (jnp.where(lo, q_pair, zero), jnp.where(lo, zero, q_pair))
        h0 = None if is_diff else 2 * (u - DIFF_HEADS)

        def block(j, carry, masked, off=off, k_ref=k_ref, vt_ref=vt_ref, qs=qs, h0=h0):
            start = pl.multiple_of(j * tk, tk)
            k = k_ref[0, pl.ds(start, tk), off:off + LANES]
            scores = [_dot_nt(k, q) for q in qs]
            out = []
            for idx in range(2):
                m, l, acc = carry[idx]
                st = scores[idx]
                if h0 is None:
                    cq = None
                else:
                    st = st - c_ref[0, pl.ds(start, tk), h0 + idx:h0 + idx + 1]
                    cq = cq_ref[0, 0, h0 + idx:h0 + idx + 1, :]
                if masked:
                    st = jnp.where(causal, st, NEG)
                mt = jnp.max(st, axis=0, keepdims=True)
                m_new = jnp.maximum(m, mt if cq is None else mt + cq)
                alpha = jnp.exp2(m - m_new)
                e = jnp.exp2(st - (m_new if cq is None else m_new - cq))
                l = alpha * l + jnp.sum(e, axis=0, keepdims=True)
                e = e.astype(BF16)
                acc = alpha * acc
                for i in range(tk // vb):
                    acc = acc + _dot(vt_ref[0, j * (tk // vb) + i, off:off + LANES, :], e[i * vb:(i + 1) * vb])
                out.append((m_new, l, acc))
            return tuple(out)

        init = tuple((jnp.full((1, tq), NEG, F32), jnp.zeros((1, tq), F32), jnp.zeros((LANES, tq), F32))
                     for _ in range(2))
        carry = lax.fori_loop(0, qi, lambda j, c: block(j, c, False), init)
        (_, la, acca), (_, lb, accb) = block(qi, carry, True)
        if is_diff:
            o = (acca / la - lam * (accb / lb)).T
            ms = jnp.mean(o * o, axis=1, keepdims=True)
            o = o * lax.rsqrt(ms + SUBLN_EPS) * g * (1.0 - lam_init)
            mix_ref[0, :, off:off + LANES] = o.astype(BF16)
        else:
            o = jnp.where(top, acca / la, accb / lb).T
            mix_ref[0, :, WIDTH + off:WIDTH + off + LANES] = o.astype(BF16)


def _prompt_attention(q, k, vt, c, cq, lamv, g, tq, lam_init):
    b, s, _ = q.shape
    n_vb, vb = vt.shape[1], vt.shape[3]
    assert tq % vb == 0
    qspec = lambda half: pl.BlockSpec((1, tq, WIDTH), lambda bi, qi: (bi, qi, half))
    kspec = lambda half: pl.BlockSpec((1, s, WIDTH), lambda bi, qi: (bi, 0, half))
    vspec = lambda half: pl.BlockSpec((1, n_vb, WIDTH, vb), lambda bi, qi: (bi, 0, half, 0))
    return pl.pallas_call(
        functools.partial(_attn_kernel, tq=tq, lam_init=lam_init),
        grid=(b, s // tq),
        in_specs=[qspec(0), qspec(1), kspec(0), kspec(1), vspec(0), vspec(1),
                  pl.BlockSpec((1, s, FOX_HEADS), lambda bi, qi: (bi, 0, 0)),
                  pl.BlockSpec((1, 1, FOX_HEADS, tq), lambda bi, qi: (bi, qi, 0, 0)),
                  pl.BlockSpec((4, HEAD_DIM), lambda bi, qi: (0, 0)),
                  pl.BlockSpec((1, LANES), lambda bi, qi: (0, 0))],
        out_specs=pl.BlockSpec((1, tq, D_MODEL), lambda bi, qi: (bi, qi, 0)),
        out_shape=jax.ShapeDtypeStruct((b, s, D_MODEL), BF16),
        compiler_params=_cparams(("arbitrary", "arbitrary")),
        name="attn",
    )(q, q, k, k, vt, vt, c, cq, lamv, g)


def _suffix_kernel(x_ref, e_ref, t_ref):
    t = lax.broadcasted_iota(jnp.int32, (LANES, LANES), 0)
    u = lax.broadcasted_iota(jnp.int32, (LANES, LANES), 1)
    x = x_ref[...]
    e_ref[...] = _dot_hi(x, (t > u).astype(F32))
    t_ref[...] = _dot_hi(x, jnp.ones((LANES, LANES), F32))


def _page_suffix(lft, tr):
    n = lft.shape[0]
    spec = pl.BlockSpec((tr, LANES), lambda i: (i, 0))
    return pl.pallas_call(
        _suffix_kernel,
        grid=(n // tr,),
        in_specs=[spec],
        out_specs=[spec, spec],
        out_shape=[jax.ShapeDtypeStruct(lft.shape, F32)] * 2,
        compiler_params=_cparams(("arbitrary",)),
        name="suffix",
    )(lft)


N_DECODE_SMALL = 9


def _decode_parts(j, n_steps, refs, pages, lam_init):
    qd_ref, qf_ref, knd_ref, vnd_ref, knf_ref, vnf_ref, slogf_ref, lamv_ref, g_ref = refs[:N_DECODE_SMALL]
    rest = refs[N_DECODE_SMALL:]
    kd = rest[0:pages]
    vd = rest[pages:2 * pages]
    kf = rest[2 * pages:3 * pages]
    vf = rest[3 * pages:4 * pages]
    ex = rest[4 * pages:5 * pages]
    tot = rest[5 * pages:6 * pages]
    od_ref, of_ref = rest[6 * pages], rest[6 * pages + 1]
    md, ld, accd, mf, lf, accf, carry = rest[6 * pages + 2:]
    rows = 2 * DIFF_HEADS
    th = DIFF_HEADS * LANES

    def init():
        @pl.when(j == 0)
        def _():
            md[...] = jnp.full(md.shape, NEG, F32)
            mf[...] = jnp.full(mf.shape, NEG, F32)
            ld[...] = jnp.zeros(ld.shape, F32)
            lf[...] = jnp.zeros(lf.shape, F32)
            accd[...] = jnp.zeros(accd.shape, F32)
            accf[...] = jnp.zeros(accf.shape, F32)
            carry[...] = jnp.zeros(carry.shape, F32)

    def queries():
        lane1 = lax.broadcasted_iota(jnp.int32, (rows, LANES), 1)
        r1 = lax.broadcasted_iota(jnp.int32, (rows, LANES), 0)
        qd = jnp.where((lane1 >> 6) == (r1 & 1), qd_ref[0], 0.0)
        col = lax.broadcasted_iota(jnp.int32, (rows, th), 1)
        r = lax.broadcasted_iota(jnp.int32, (rows, th), 0)
        own_head = (col & (DIFF_HEADS - 1)) == (r >> 1)
        half_mask = (col >> 6) == r
        qf = jnp.where(half_mask, qf_ref[0], 0.0)
        return qd, qf, own_head, half_mask, r1

    def update(m_ref, l_ref, acc_ref, s_list, pv_fn):
        s = jnp.concatenate(s_list, axis=1)
        w = s.shape[1] // pages
        m_old = m_ref[...]
        m_new = jnp.maximum(m_old, jnp.max(s, axis=1, keepdims=True))
        alpha = jnp.exp2(m_old - m_new)
        e = jnp.exp2(s - m_new)
        l_ref[...] = alpha * l_ref[...] + jnp.sum(e, axis=1, keepdims=True)
        pv = pv_fn(e[:, 0:w], 0)
        for k in range(1, pages):
            pv = pv + pv_fn(e[:, k * w:(k + 1) * w], k)
        acc_ref[...] = alpha * acc_ref[...] + pv
        m_ref[...] = m_new

    def main():
        qd, qf, own_head, _, _ = queries()
        sd = [jnp.where(own_head, _dot_nt(qd, kd[k][...]), NEG) for k in range(pages)]
        update(md, ld, accd, sd, lambda e, k: _dot(e, vd[k][...]))

        run = carry[...]
        base = slogf_ref[0]
        sf = [None] * pages
        for k in reversed(range(pages)):
            sf[k] = _dot(qf, kf[k][...]) + LOG2E * ((base + run) + ex[k][...])
            run = run + tot[k][...]
        carry[...] = run
        update(mf, lf, accf, sf, lambda e, k: _dot_nt(e, vf[k][...]))

    def final():
        @pl.when(j == n_steps - 1)
        def _():
            qd, qf, _, half_mask, r1 = queries()
            lam = _diff_lambda(lamv_ref[...], lam_init)

            def finish(m_ref, l_ref, acc_ref, q8, k_new, v_new):
                s_new = jnp.sum(q8 * k_new, axis=1, keepdims=True)
                m_old = m_ref[...]
                m_fin = jnp.maximum(m_old, s_new)
                a = jnp.exp2(m_old - m_fin)
                en = jnp.exp2(s_new - m_fin)
                l_fin = a * l_ref[...] + en
                return (a * acc_ref[...] + en * v_new) / l_fin

            od8 = finish(md, ld, accd, qd, knd_ref[0], vnd_ref[0])
            comb = od8 * jnp.where((r1 & 1) == 0, 1.0, -lam)
            o = comb + pltpu.roll(comb, rows - 1, 0)
            ms = jnp.mean(o * o, axis=1, keepdims=True)
            od_ref[0] = o * lax.rsqrt(ms + SUBLN_EPS) * g_ref[...] * (1.0 - lam_init)
            of8 = finish(mf, lf, accf, qf, knf_ref[0], vnf_ref[0])
            of_ref[0] = jnp.sum(jnp.where(half_mask, of8, 0.0), axis=0, keepdims=True)

    return init, main, final


def _decode_kernel(pt_ref, *refs, pages, lam_init):
    del pt_ref
    for part in _decode_parts(pl.program_id(1), pl.num_programs(1), refs, pages, lam_init):
        part()


def _decode_attention(page_table, dec_args, pages, lam_init):
    db, n_pages = page_table.shape
    n_steps = n_pages // pages
    rows = FOX_HEADS

    def page_map(p):
        return lambda b, j, pt: (pt[b, (n_steps - 1 - j) * pages + p], 0, 0)

    per_seq = lambda b, j, pt: (b, 0, 0)
    const = lambda b, j, pt: (0, 0)
    r128 = pl.BlockSpec((1, rows, LANES), per_seq)
    v512 = pl.BlockSpec((1, 1, WIDTH), per_seq)
    in_specs = [r128, v512, r128, r128, v512, v512,
                pl.BlockSpec((1, FOX_HEADS, 1), per_seq),
                pl.BlockSpec((4, HEAD_DIM), const),
                pl.BlockSpec((1, LANES), const)]
    assert len(in_specs) == N_DECODE_SMALL
    for _ in range(4):
        in_specs += [pl.BlockSpec((None, WIDTH, LANES), page_map(p)) for p in range(pages)]
    for _ in range(2):
        in_specs += [pl.BlockSpec((None, FOX_HEADS, LANES), page_map(p)) for p in range(pages)]
    grid_spec = pltpu.PrefetchScalarGridSpec(
        num_scalar_prefetch=1,
        grid=(db, n_steps),
        in_specs=in_specs,
        out_specs=[r128, v512],
        scratch_shapes=[pltpu.VMEM((rows, 1), F32), pltpu.VMEM((rows, 1), F32), pltpu.VMEM((rows, LANES), F32),
                        pltpu.VMEM((rows, 1), F32), pltpu.VMEM((rows, 1), F32), pltpu.VMEM((rows, WIDTH), F32),
                        pltpu.VMEM((rows, LANES), F32)],
    )
    pools = dec_args[N_DECODE_SMALL:]
    args = list(dec_args[:N_DECODE_SMALL]) + [p for pool in pools for p in [pool] * pages]
    return pl.pallas_call(
        functools.partial(_decode_kernel, pages=pages, lam_init=lam_init),
        grid_spec=grid_spec,
        out_shape=[jax.ShapeDtypeStruct((db, rows, LANES), F32), jax.ShapeDtypeStruct((db, 1, WIDTH), F32)],
        compiler_params=_cparams(("arbitrary", "arbitrary")),
        name="decode",
    )(page_table, *args)


ROUTE_MEMBER = 3 * N_EXPERTS
GROUP_ROWS = 16


def _tail_kernel(mix_ref, x_ref, wo_ref, g_ref, b_ref, wr_ref, br_ref, h_ref, hb_ref, gate_ref, route_ref,
                 *, alpha, split):
    y = _dot(mix_ref[...].astype(BF16), wo_ref[...])
    h = _layer_norm(alpha * x_ref[...] + y, g_ref[...], b_ref[...])
    h_ref[...] = h
    hb = h.astype(BF16)
    hb_ref[...] = hb
    if split:
        two = _dot(hb, wr_ref[...])
        h_lo = (h - hb.astype(F32)).astype(BF16)
        logits = (two[:, :LANES] + two[:, LANES:]) + _dot(h_lo, wr_ref[:, :LANES]) + br_ref[...]
    else:
        logits = _dot_hi(h, wr_ref[...]) + br_ref[...]
    tm = logits.shape[0]
    lane = lax.broadcasted_iota(jnp.int32, (tm, LANES), 1).astype(F32)
    big = float(LANES)
    gmask = lane < N_GROUPS
    gl = jnp.where(gmask, logits, NEG)
    gmax = jnp.max(gl, axis=1, keepdims=True)
    gsum = jnp.sum(jnp.where(gmask, jnp.exp(gl - gmax), 0.0), axis=1, keepdims=True)
    g_val = 1.0 / gsum
    g_idx = jnp.min(jnp.where(gmask & (gl == gmax), lane, big), axis=1, keepdims=True)
    first = N_GROUPS + EXPERTS_PER_GROUP * g_idx
    emask = (lane >= first) & (lane < first + EXPERTS_PER_GROUP)
    el = jnp.where(emask, logits, NEG)
    v1 = jnp.max(el, axis=1, keepdims=True)
    i1 = jnp.min(jnp.where(emask & (el == v1), lane, big), axis=1, keepdims=True)
    el2 = jnp.where(lane == i1, NEG, el)
    v2 = jnp.max(el2, axis=1, keepdims=True)
    i2 = jnp.min(jnp.where(emask & (el2 == v2) & (lane != i1), lane, big), axis=1, keepdims=True)
    t = jnp.exp(v2 - v1)
    w1 = g_val / (1.0 + t)
    w2 = g_val * t / (1.0 + t)
    gates = jnp.where(lane == i1, w1, 0.0) + jnp.where(lane == i2, w2, 0.0)
    gate_ref[...] = gates
    hi = gates.astype(BF16).astype(F32)
    mid = (gates - hi).astype(BF16).astype(F32)
    lo = (gates - hi - mid).astype(BF16).astype(F32)
    route = (pltpu.roll(hi, LANES - N_GROUPS, 1) + pltpu.roll(mid, N_EXPERTS - N_GROUPS, 1)
             + pltpu.roll(lo, 2 * N_EXPERTS - N_GROUPS, 1))
    route = jnp.where(lane == ROUTE_MEMBER + g_idx, 1.0, route)
    route_ref[...] = route.astype(BF16)


def _tail(mix, x2d, wo_bf, g1, b1, wr, br_pad, tm, alpha):
    n = x2d.shape[0]
    row = lambda i: (i, 0)
    const = lambda i: (0, 0)
    return pl.pallas_call(
        functools.partial(_tail_kernel, alpha=alpha, split=wr.dtype == BF16),
        grid=(n // tm,),
        in_specs=[pl.BlockSpec((tm, D_MODEL), row), pl.BlockSpec((tm, D_MODEL), row),
                  pl.BlockSpec((D_MODEL, D_MODEL), const),
                  pl.BlockSpec((1, D_MODEL), const), pl.BlockSpec((1, D_MODEL), const),
                  pl.BlockSpec(wr.shape, const), pl.BlockSpec((1, LANES), const)],
        out_specs=[pl.BlockSpec((tm, D_MODEL), row), pl.BlockSpec((tm, D_MODEL), row),
                   pl.BlockSpec((tm, LANES), row), pl.BlockSpec((tm, LANES), row)],
        out_shape=[jax.ShapeDtypeStruct((n, D_MODEL), F32), jax.ShapeDtypeStruct((n, D_MODEL), BF16),
                   jax.ShapeDtypeStruct((n, LANES), F32), jax.ShapeDtypeStruct((n, LANES), BF16)],
        compiler_params=_cparams(("arbitrary",)),
        name="tail",
    )(mix, x2d, wo_bf, g1, b1, wr, br_pad)


def _swiglu_rows(x, gate, wg_ref, wu_ref, wd_ref):
    a = _dot(x, wg_ref[0])
    u = _dot(x, wu_ref[0])
    hmid = (a * jax.nn.sigmoid(a)) * u * gate
    return _dot(hmid.astype(BF16), wd_ref[0])


def _expert_specs(ix):
    return [pl.BlockSpec((1, D_MODEL, D_EXPERT), ix), pl.BlockSpec((1, D_MODEL, D_EXPERT), ix),
            pl.BlockSpec((1, D_EXPERT, D_MODEL), ix)]


def _moe_kernel(hb_ref, h_ref, gate_ref, wg_ref, wu_ref, wd_ref, g_ref, b_ref, y_ref, acc_ref, *, alpha):
    e = pl.program_id(1)

    @pl.when(e == 0)
    def _():
        acc_ref[...] = jnp.zeros(acc_ref.shape, F32)

    xb = hb_ref[...]
    lane = lax.broadcasted_iota(jnp.int32, (xb.shape[0], LANES), 1)
    gate = jnp.sum(jnp.where(lane == e + N_GROUPS, gate_ref[...], 0.0), axis=1, keepdims=True)
    acc_ref[...] += _swiglu_rows(xb, gate, wg_ref, wu_ref, wd_ref)

    @pl.when(e == N_EXPERTS - 1)
    def _():
        y_ref[...] = _layer_norm(alpha * h_ref[...] + acc_ref[...], g_ref[...], b_ref[...])


def _moe(hb, h, gates, wg_bf, wu_bf, wd_bf, g2, b2, tm, alpha):
    n = h.shape[0]
    row = lambda i, e: (i, 0)
    const = lambda i, e: (0, 0)
    return pl.pallas_call(
        functools.partial(_moe_kernel, alpha=alpha),
        grid=(n // tm, N_EXPERTS),
        in_specs=[pl.BlockSpec((tm, D_MODEL), row), pl.BlockSpec((tm, D_MODEL), row),
                  pl.BlockSpec((tm, LANES), row)] + _expert_specs(lambda i, e: (e, 0, 0))
                 + [pl.BlockSpec((1, D_MODEL), const), pl.BlockSpec((1, D_MODEL), const)],
        out_specs=pl.BlockSpec((tm, D_MODEL), row),
        out_shape=jax.ShapeDtypeStruct((n, D_MODEL), F32),
        scratch_shapes=[pltpu.VMEM((tm, D_MODEL), F32)],
        compiler_params=_cparams(("arbitrary", "arbitrary")),
        name="moe",
    )(hb, h, gates, wg_bf, wu_bf, wd_bf, g2, b2)


def _route_gate(route, lane, e):
    pick = (lane == e) | (lane == e + N_EXPERTS) | (lane == e + 2 * N_EXPERTS)
    return jnp.sum(jnp.where(pick, route, 0.0), axis=1, keepdims=True)


def _moe_grouped_kernel(hb_ref, h_ref, route_ref, wg_ref, wu_ref, wd_ref, g_ref, b_ref, y_ref,
                        acc_ref, rcol_ref, rrow_ref, mrow_ref, xg_ref, gsel_ref, yg_ref, cnt_ref,
                        *, alpha, cap):
    e = pl.program_id(1)
    grp = e // EXPERTS_PER_GROUP
    ts = hb_ref.shape[0]
    cap_pad = yg_ref.shape[0]
    lane = lax.broadcasted_iota(jnp.int32, (ts, LANES), 1)

    @pl.when(e == 0)
    def _():
        acc_ref[...] = jnp.zeros(acc_ref.shape, F32)
        t = lax.broadcasted_iota(jnp.int32, (LANES, LANES), 0)
        u = lax.broadcasted_iota(jnp.int32, (LANES, LANES), 1)
        lower = (u < t).astype(BF16)
        carry_c = jnp.zeros((1, LANES), F32)
        carry_r = jnp.zeros((GROUP_ROWS, 1), F32)
        for blk in range(ts // LANES):
            sl = slice(blk * LANES, (blk + 1) * LANES)
            memb = jnp.where((u >= ROUTE_MEMBER) & (u < ROUTE_MEMBER + N_GROUPS),
                             route_ref[sl, :].astype(F32), 0.0)
            rcol_ref[sl, :] = _dot(lower, memb.astype(BF16)) + carry_c
            carry_c = carry_c + jnp.sum(memb, axis=0, keepdims=True)
            mrow = memb.T[ROUTE_MEMBER:ROUTE_MEMBER + GROUP_ROWS, :]
            mrow_ref[:, sl] = mrow
            rrow_ref[:, sl] = _dot_nt(mrow.astype(BF16), lower) + carry_r
            carry_r = carry_r + jnp.sum(mrow, axis=1, keepdims=True)
        lane1 = lax.broadcasted_iota(jnp.int32, (1, LANES), 1)
        for gi in range(N_GROUPS):
            cnt_ref[gi] = jnp.sum(jnp.where(lane1 == ROUTE_MEMBER + gi, carry_c, 0.0)).astype(jnp.int32)

    small = cnt_ref[grp] <= cap

    @pl.when((e % EXPERTS_PER_GROUP == 0) & small)
    def _():
        pos = lax.broadcasted_iota(jnp.int32, (cap, ts), 0).astype(F32)
        rank = jnp.where(mrow_ref[pl.ds(grp, 1), :] > 0.5, rrow_ref[pl.ds(grp, 1), :], -1.0)
        onehot = jnp.where(rank == pos, 1.0, 0.0).astype(BF16)
        xg_ref[...] = _dot(onehot, hb_ref[...]).astype(BF16)
        gsel_ref[...] = _dot(onehot, route_ref[...])
        yg_ref[...] = jnp.zeros(yg_ref.shape, F32)

    @pl.when(small)
    def _():
        lane_c = lax.broadcasted_iota(jnp.int32, (cap, LANES), 1)
        gate = _route_gate(gsel_ref[...], lane_c, e)
        yg_ref[0:cap, :] += _swiglu_rows(xg_ref[...], gate, wg_ref, wu_ref, wd_ref)

    @pl.when(jnp.logical_not(small))
    def _():
        gate = _route_gate(route_ref[...].astype(F32), lane, e)
        acc_ref[...] += _swiglu_rows(hb_ref[...], gate, wg_ref, wu_ref, wd_ref)

    @pl.when((e % EXPERTS_PER_GROUP == EXPERTS_PER_GROUP - 1) & small)
    def _():
        mine = lane == ROUTE_MEMBER + grp
        rank = jnp.sum(jnp.where(mine, rcol_ref[...], 0.0), axis=1, keepdims=True)
        member = jnp.sum(jnp.where(mine, route_ref[...].astype(F32), 0.0), axis=1, keepdims=True)
        pos = lax.broadcasted_iota(jnp.int32, (ts, cap_pad), 1).astype(F32)
        rank = jnp.where(member > 0.5, rank, -1.0)
        onehot_t = jnp.where(rank == pos, 1.0, 0.0).astype(BF16)
        acc_ref[...] += _dot(onehot_t, yg_ref[...].astype(BF16))

    @pl.when(e == N_EXPERTS - 1)
    def _():
        y_ref[...] = _layer_norm(alpha * h_ref[...] + acc_ref[...], g_ref[...], b_ref[...])


def _moe_grouped(hb, h, route, wg_bf, wu_bf, wd_bf, g2, b2, ts, cap, alpha):
    n = h.shape[0]
    cap_pad = -(-cap // LANES) * LANES
    row = lambda i, e: (i, 0)
    const = lambda i, e: (0, 0)
    return pl.pallas_call(
        functools.partial(_moe_grouped_kernel, alpha=alpha, cap=cap),
        grid=(n // ts, N_EXPERTS),
        in_specs=[pl.BlockSpec((ts, D_MODEL), row), pl.BlockSpec((ts, D_MODEL), row),
                  pl.BlockSpec((ts, LANES), row)] + _expert_specs(lambda i, e: (e, 0, 0))
                 + [pl.BlockSpec((1, D_MODEL), const), pl.BlockSpec((1, D_MODEL), const)],
        out_specs=pl.BlockSpec((ts, D_MODEL), row),
        out_shape=jax.ShapeDtypeStruct((n, D_MODEL), F32),
        scratch_shapes=[pltpu.VMEM((ts, D_MODEL), F32),
                        pltpu.VMEM((ts, LANES), F32),
                        pltpu.VMEM((GROUP_ROWS, ts), F32),
                        pltpu.VMEM((GROUP_ROWS, ts), F32),
                        pltpu.VMEM((cap, D_MODEL), BF16),
                        pltpu.VMEM((cap, LANES), F32),
                        pltpu.VMEM((cap_pad, D_MODEL), F32),
                        pltpu.SMEM((N_GROUPS,), jnp.int32)],
        compiler_params=_cparams(("arbitrary", "arbitrary")),
        name="moe_grouped",
    )(hb, h, route, wg_bf, wu_bf, wd_bf, g2, b2)


def _rope_tables(pos):
    half = HEAD_DIM // 2
    inv = ROPE_THETA ** (-jnp.arange(half, dtype=F32) * 2.0 / HEAD_DIM)
    ang = pos.astype(F32)[:, None] * inv[None, :]
    cos, sin = jnp.cos(ang), jnp.sin(ang)
    reps = WIDTH // HEAD_DIM
    cos_t = jnp.tile(jnp.concatenate([cos, cos], axis=1), (1, reps))
    sin_t = jnp.tile(jnp.concatenate([-sin, sin], axis=1), (1, reps))
    return cos_t, sin_t


def kernel(x_prompt, x_sample, cache_diff_k, cache_diff_v, cache_fox_k, cache_fox_v, cache_fox_logf, page_table,
           w_in, b_forget, lambda_q1, lambda_k1, lambda_q2, lambda_k2, subln_g, w_out, ln1_g, ln1_b,
           w_router_group, b_router_group, w_router_expert, b_router_expert, w_gate, w_up, w_down, ln2_g, ln2_b):
    B, S, D = x_prompt.shape
    DB, T, _ = x_sample.shape
    depth = w_in.shape[0]
    assert depth == 1 and T == 1 and D == D_MODEL
    n_pool, page = cache_diff_k.shape[1], cache_diff_k.shape[2]
    n_pages = page_table.shape[1]
    past_len = n_pages * page
    assert page == LANES
    alpha = (2 * depth) ** 0.25
    l = 0
    lam_init = 0.8 - 0.6 * math.exp(-0.3 * l)

    w_pad = jnp.pad(w_in[l], ((0, 0), (0, 6 * WIDTH + LANES - w_in.shape[2]))).astype(BF16)
    bf_pad = jnp.pad(b_forget[l], (0, LANES - FOX_HEADS)).reshape(1, LANES)
    lamv = jnp.stack([lambda_q1[l], lambda_k1[l], lambda_q2[l], lambda_k2[l]])
    g1 = subln_g[l].reshape(1, LANES)
    wo_bf = w_out[l].astype(BF16)
    wr_pad = jnp.pad(jnp.concatenate([w_router_group[l], w_router_expert[l]], axis=1),
                     ((0, 0), (0, LANES - N_GROUPS - N_EXPERTS)))
    br_pad = jnp.pad(jnp.concatenate([b_router_group[l], b_router_expert[l]]),
                     (0, LANES - N_GROUPS - N_EXPERTS)).reshape(1, LANES)
    wg_bf, wu_bf, wd_bf = w_gate[l].astype(BF16), w_up[l].astype(BF16), w_down[l].astype(BF16)
    ln1g, ln1b = ln1_g[l].reshape(1, D), ln1_b[l].reshape(1, D)
    ln2g, ln2b = ln2_g[l].reshape(1, D), ln2_b[l].reshape(1, D)
    p_cos, p_sin = _rope_tables(jnp.arange(S, dtype=jnp.int32))
    s_cos, s_sin = _rope_tables(jnp.full((DB,), past_len, dtype=jnp.int32))

    tq, tp = ATTN_TILE, PROJ_TILE
    xp2 = x_prompt.reshape(B * S, D)
    (q_p, k_p, dk, dv, fk, fv, logf, vt_p) = _project(xp2, w_pad, p_cos, p_sin, bf_pad, tp, S // tp, True)
    logf_t = jnp.swapaxes(logf.reshape(B, S, FOX_HEADS), 1, 2).reshape(B * FOX_HEADS, S)
    c_t = _cumsum_rows(logf_t).reshape(B, FOX_HEADS, S)
    c = jnp.swapaxes(c_t, 1, 2)
    c_blk = jnp.swapaxes(c_t.reshape(B, FOX_HEADS, S // tq, tq), 1, 2)
    mix_p = _prompt_attention(q_p.reshape(B, S, 2 * WIDTH), k_p.reshape(B, S, 2 * WIDTH),
                              vt_p.reshape(B, S // tp, 2 * WIDTH, tp), c, c_blk, lamv, g1, tq, lam_init)
    wr_hi = wr_pad.astype(BF16)
    wr_split = jnp.concatenate([wr_hi, (wr_pad - wr_hi.astype(F32)).astype(BF16)], axis=1)
    h_p, hb_p, _, route_p = _tail(mix_p.reshape(B * S, D), xp2, wo_bf, ln1g, ln1b, wr_split, br_pad, TAIL_TILE, alpha)
    y_p = _moe_grouped(hb_p, h_p, route_p, wg_bf, wu_bf, wd_bf, ln2g, ln2b, MOE_TILE, MOE_GROUP_CAP, alpha)

    xs2 = x_sample.reshape(DB, D)
    (q_s, _, sdk, sdv, sfk, sfv, slogf) = _project(xs2, w_pad, s_cos, s_sin, bf_pad, DB, 1, False)
    sdq, sfq = q_s[:, :WIDTH], q_s[:, WIDTH:]
    diff_pool = lambda a: a[l].reshape(n_pool, page * DIFF_HEADS, 2 * HEAD_DIM)
    fox_pool = lambda a: jnp.transpose(a[l], (0, 2, 3, 1)).reshape(n_pool, FOX_HEADS * HEAD_DIM, page)
    lft = jnp.transpose(cache_fox_logf[l], (0, 2, 1)).reshape(n_pool * FOX_HEADS, page)
    ex, tot = _page_suffix(lft, 4096)
    rep = lambda a: jnp.repeat(a.astype(F32).reshape(DB, DIFF_HEADS, 2 * HEAD_DIM), 2, axis=1)
    v3 = lambda a: a.astype(F32).reshape(DB, 1, WIDTH)
    dec_args = [rep(sdq), v3(sfq), rep(sdk), rep(sdv), v3(sfk), v3(sfv), slogf.reshape(DB, FOX_HEADS, 1), lamv, g1,
                diff_pool(cache_diff_k), diff_pool(cache_diff_v), fox_pool(cache_fox_k), fox_pool(cache_fox_v),
                ex.reshape(n_pool, FOX_HEADS, page), tot.reshape(n_pool, FOX_HEADS, page)]
    od_s, of_s = _decode_attention(page_table, dec_args, DECODE_PAGES, lam_init)
    mix_s = jnp.concatenate([od_s[:, 0::2, :].reshape(DB, WIDTH), of_s.reshape(DB, WIDTH)], axis=1)
    h_s, hb_s, gates_s, _ = _tail(mix_s, xs2, wo_bf, ln1g, ln1b, wr_pad, br_pad, DB, alpha)
    y_s = _moe(hb_s, h_s, gates_s, wg_bf, wu_bf, wd_bf, ln2g, ln2b, DB, alpha)

    fox_out = lambda a: jnp.transpose(a.reshape(B, FOX_HEADS, HEAD_DIM, S), (0, 3, 1, 2))[None]
    return (y_p.reshape(B, S, D), y_s.reshape(DB, T, D),
            dk.reshape(1, B, S, DIFF_HEADS, 2 * HEAD_DIM), dv.reshape(1, B, S, DIFF_HEADS, 2 * HEAD_DIM),
            fox_out(fk), fox_out(fv),
            logf.reshape(1, B, S, FOX_HEADS),
            sdk.reshape(1, DB, T, DIFF_HEADS, 2 * HEAD_DIM), sdv.reshape(1, DB, T, DIFF_HEADS, 2 * HEAD_DIM),
            sfk.reshape(1, DB, T, FOX_HEADS, HEAD_DIM), sfv.reshape(1, DB, T, FOX_HEADS, HEAD_DIM),
            slogf.reshape(1, DB, T, FOX_HEADS))
```

```python
import functools
import math

import jax
import jax.numpy as jnp
from jax import lax
from jax.experimental import pallas as pl
from jax.experimental.pallas import tpu as pltpu

F32 = jnp.float32
BF16 = jnp.bfloat16

HEAD_DIM = 64
DIFF_HEADS = 4
FOX_HEADS = 8
WIDTH = 512
D_MODEL = 1024
N_GROUPS = 4
EXPERTS_PER_GROUP = 4
N_EXPERTS = 16
D_EXPERT = 512
ROPE_THETA = 10000.0
LN_EPS = 1e-5
SUBLN_EPS = 1e-5
LANES = 128
NEG = -1e30
LOG2E = 1.4426950408889634
V7X_VMEM_LIMIT = 48 * 1024 * 1024
ATTN_TILE = 512
PROJ_TILE = 512
DECODE_PAGES = 16
TAIL_TILE = 512
MOE_TILE = 1024
MOE_GROUP_CAP = 288


def _cparams(sem):
    return pltpu.CompilerParams(dimension_semantics=sem, vmem_limit_bytes=V7X_VMEM_LIMIT)


def _dot(a, b):
    return jnp.dot(a, b, preferred_element_type=F32)


def _dot_nt(a, b):
    return lax.dot_general(a, b, (((1,), (1,)), ((), ())), preferred_element_type=F32)


def _dot_hi(a, b):
    return jnp.dot(a, b, preferred_element_type=F32, precision=lax.Precision.HIGHEST)


def _layer_norm(x, g, b):
    mu = jnp.mean(x, axis=-1, keepdims=True)
    xc = x - mu
    var = jnp.mean(xc * xc, axis=-1, keepdims=True)
    return xc * lax.rsqrt(var + LN_EPS) * g + b


def _diff_lambda(lv, lam_init):
    return (jnp.exp(jnp.sum(lv[0:1] * lv[1:2], axis=1, keepdims=True))
            - jnp.exp(jnp.sum(lv[2:3] * lv[3:4], axis=1, keepdims=True)) + lam_init)


def _proj_kernel(x_ref, w_ref, cos_ref, sin_ref, bf_ref,
                 q_ref, kb_ref, dk_ref, dv_ref, fk_ref, fv_ref, logf_ref, *vt_refs):
    tm = x_ref.shape[0]
    xb = x_ref[...].astype(BF16)
    cos = cos_ref[...]
    sin = sin_ref[...]
    lane = lax.broadcasted_iota(jnp.int32, (tm, WIDTH), 1)
    first_half = (lane & (HEAD_DIM // 2)) == 0
    qscale = HEAD_DIM ** -0.5 * LOG2E

    def mm(c):
        return _dot(xb, w_ref[:, c * WIDTH:(c + 1) * WIDTH])

    def rope(p):
        partner = jnp.where(first_half,
                            pltpu.roll(p, WIDTH - HEAD_DIM // 2, 1),
                            pltpu.roll(p, HEAD_DIM // 2, 1))
        return p * cos + partner * sin

    dq = rope(mm(0))
    q_ref[:, :WIDTH] = (dq * qscale).astype(BF16)
    dk = rope(mm(1))
    for hh in range(DIFF_HEADS):
        dk_ref[:, hh, :] = dk[:, hh * LANES:(hh + 1) * LANES]
    kb_ref[:, :WIDTH] = dk.astype(BF16)
    dv = mm(2)
    for hh in range(DIFF_HEADS):
        dv_ref[:, hh, :] = dv[:, hh * LANES:(hh + 1) * LANES]
    q_ref[:, WIDTH:] = (mm(3) * qscale).astype(BF16)
    fk = mm(4)
    kb_ref[:, WIDTH:] = fk.astype(BF16)
    fv = mm(5)
    if vt_refs:
        fvt = fv.T
        fk_ref[0] = fk.T
        fv_ref[0] = fvt
        vt_refs[0][0, :WIDTH, :] = dv.T.astype(BF16)
        vt_refs[0][0, WIDTH:, :] = fvt.astype(BF16)
    else:
        fk_ref[...] = fk
        fv_ref[...] = fv
    z = _dot(xb, w_ref[:, 6 * WIDTH:6 * WIDTH + LANES]) + bf_ref[...]
    logf = jnp.minimum(z, 0.0) - jnp.log1p(jnp.exp(-jnp.abs(z)))
    logf_ref[...] = logf[:, :FOX_HEADS]


def _project(x2d, w_bf, cos_t, sin_t, bf_pad, tm, table_blocks, prompt_layout):
    n = x2d.shape[0]
    grid = (n // tm,)
    row = lambda i: (i, 0)
    tab = lambda i: (i % table_blocks, 0)
    const = lambda i: (0, 0)
    wide = pl.BlockSpec((tm, WIDTH), row)
    both = pl.BlockSpec((tm, 2 * WIDTH), row)
    out_shape = ([jax.ShapeDtypeStruct((n, 2 * WIDTH), BF16)] * 2
                 + [jax.ShapeDtypeStruct((n, DIFF_HEADS, LANES), F32)] * 2
                 + [jax.ShapeDtypeStruct((n, WIDTH), F32)] * 2
                 + [jax.ShapeDtypeStruct((n, FOX_HEADS), F32)])
    head3 = pl.BlockSpec((tm, DIFF_HEADS, LANES), lambda i: (i, 0, 0))
    out_specs = [both, both, head3, head3, wide, wide, pl.BlockSpec((tm, FOX_HEADS), row)]
    if prompt_layout:
        n_seq = n // (table_blocks * tm)
        by_seq = pl.BlockSpec((1, WIDTH, tm), lambda i: (i // table_blocks, 0, i % table_blocks))
        out_shape[4:6] = [jax.ShapeDtypeStruct((n_seq, WIDTH, table_blocks * tm), F32)] * 2
        out_specs[4:6] = [by_seq, by_seq]
        out_shape += [jax.ShapeDtypeStruct((n // tm, 2 * WIDTH, tm), BF16)]
        out_specs += [pl.BlockSpec((1, 2 * WIDTH, tm), lambda i: (i, 0, 0))]
    return pl.pallas_call(
        _proj_kernel,
        grid=grid,
        in_specs=[pl.BlockSpec((tm, D_MODEL), row),
                  pl.BlockSpec(w_bf.shape, const),
                  pl.BlockSpec((tm, WIDTH), tab),
                  pl.BlockSpec((tm, WIDTH), tab),
                  pl.BlockSpec((1, LANES), const)],
        out_specs=out_specs,
        out_shape=out_shape,
        compiler_params=_cparams(("arbitrary",)),
        name="proj",
    )(x2d, w_bf, cos_t, sin_t, bf_pad)


def _cumsum_kernel(x_ref, o_ref):
    rows, s = x_ref.shape
    t = lax.broadcasted_iota(jnp.int32, (LANES, LANES), 0)
    u = lax.broadcasted_iota(jnp.int32, (LANES, LANES), 1)
    tri = (t <= u).astype(F32)
    carry = jnp.zeros((rows, 1), F32)
    for ch in range(s // LANES):
        blk = _dot_hi(x_ref[:, ch * LANES:(ch + 1) * LANES], tri) + carry
        o_ref[:, ch * LANES:(ch + 1) * LANES] = blk * LOG2E
        carry = blk[:, LANES - 1:LANES]


def _cumsum_rows(xt):
    return pl.pallas_call(
        _cumsum_kernel,
        out_shape=jax.ShapeDtypeStruct(xt.shape, F32),
        name="cumsum",
    )(xt)


def _attn_kernel(dq_ref, fq_ref, dk_ref, fk_ref, dvt_ref, fvt_ref, c_ref, cq_ref, lamv_ref, g_ref,
                 mix_ref, *, tq, lam_init):
    qi = pl.program_id(1)
    tk = tq
    lane = lax.broadcasted_iota(jnp.int32, (tq, LANES), 1)
    lo = lane < HEAD_DIM
    vb = dvt_ref.shape[3]
    causal = lax.broadcasted_iota(jnp.int32, (tk, tq), 0) <= lax.broadcasted_iota(jnp.int32, (tk, tq), 1)
    top = lax.broadcasted_iota(jnp.int32, (LANES, tq), 0) < HEAD_DIM
    lam = _diff_lambda(lamv_ref[...], lam_init)
    g = g_ref[...]

    for u in range(DIFF_HEADS + FOX_HEADS // 2):
        is_diff = u < DIFF_HEADS
        off = (u if is_diff else u - DIFF_HEADS) * LANES
        k_ref, vt_ref = (dk_ref, dvt_ref) if is_diff else (fk_ref, fvt_ref)
        q_pair = (dq_ref if is_diff else fq_ref)[0, :, off:off + LANES]
        zero = jnp.zeros_like(q_pair)
        qs = (jnp.where(lo, q_pair, zero), jnp.where(lo, zero, q_pair))
        h0 = None if is_diff else 2 * (u - DIFF_HEADS)

        def block(j, carry, masked, off=off, k_ref=k_ref, vt_ref=vt_ref, qs=qs, h0=h0):
            start = pl.multiple_of(j * tk, tk)
            k = k_ref[0, pl.ds(start, tk), off:off + LANES]
            scores = [_dot_nt(k, q) for q in qs]
            out = []
            for idx in range(2):
                m, l, acc = carry[idx]
                st = scores[idx]
                if h0 is None:
                    cq = None
                else:
                    st = st - c_ref[0, pl.ds(start, tk), h0 + idx:h0 + idx + 1]
                    cq = cq_ref[0, 0, h0 + idx:h0 + idx + 1, :]
                if masked:
                    st = jnp.where(causal, st, NEG)
                mt = jnp.max(st, axis=0, keepdims=True)
                m_new = jnp.maximum(m, mt if cq is None else mt + cq)
                alpha = jnp.exp2(m - m_new)
                e = jnp.exp2(st - (m_new if cq is None else m_new - cq))
                l = alpha * l + jnp.sum(e, axis=0, keepdims=True)
                e = e.astype(BF16)
                acc = alpha * acc
                for i in range(tk // vb):
                    acc = acc + _dot(vt_ref[0, j * (tk // vb) + i, off:off + LANES, :], e[i * vb:(i + 1) * vb])
                out.append((m_new, l, acc))
            return tuple(out)

        init = tuple((jnp.full((1, tq), NEG, F32), jnp.zeros((1, tq), F32), jnp.zeros((LANES, tq), F32))
                     for _ in range(2))
        carry = lax.fori_loop(0, qi, lambda j, c: block(j, c, False), init)
        (_, la, acca), (_, lb, accb) = block(qi, carry, True)
        if is_diff:
            o = (acca / la - lam * (accb / lb)).T
            ms = jnp.mean(o * o, axis=1, keepdims=True)
            o = o * lax.rsqrt(ms + SUBLN_EPS) * g * (1.0 - lam_init)
            mix_ref[0, :, off:off + LANES] = o.astype(BF16)
        else:
            o = jnp.where(top, acca / la, accb / lb).T
            mix_ref[0, :, WIDTH + off:WIDTH + off + LANES] = o.astype(BF16)


def _prompt_attention(q, k, vt, c, cq, lamv, g, tq, lam_init):
    b, s, _ = q.shape
    n_vb, vb = vt.shape[1], vt.shape[3]
    assert tq % vb == 0
    qspec = lambda half: pl.BlockSpec((1, tq, WIDTH), lambda bi, qi: (bi, qi, half))
    kspec = lambda half: pl.BlockSpec((1, s, WIDTH), lambda bi, qi: (bi, 0, half))
    vspec = lambda half: pl.BlockSpec((1, n_vb, WIDTH, vb), lambda bi, qi: (bi, 0, half, 0))
    return pl.pallas_call(
        functools.partial(_attn_kernel, tq=tq, lam_init=lam_init),
        grid=(b, s // tq),
        in_specs=[qspec(0), qspec(1), kspec(0), kspec(1), vspec(0), vspec(1),
                  pl.BlockSpec((1, s, FOX_HEADS), lambda bi, qi: (bi, 0, 0)),
                  pl.BlockSpec((1, 1, FOX_HEADS, tq), lambda bi, qi: (bi, qi, 0, 0)),
                  pl.BlockSpec((4, HEAD_DIM), lambda bi, qi: (0, 0)),
                  pl.BlockSpec((1, LANES), lambda bi, qi: (0, 0))],
        out_specs=pl.BlockSpec((1, tq, D_MODEL), lambda bi, qi: (bi, qi, 0)),
        out_shape=jax.ShapeDtypeStruct((b, s, D_MODEL), BF16),
        compiler_params=_cparams(("arbitrary", "arbitrary")),
        name="attn",
    )(q, q, k, k, vt, vt, c, cq, lamv, g)


def _suffix_kernel(x_ref, s_ref):
    t = lax.broadcasted_iota(jnp.int32, (LANES, LANES), 0)
    u = lax.broadcasted_iota(jnp.int32, (LANES, LANES), 1)
    s_ref[...] = _dot_hi(x_ref[...], (t >= u).astype(F32))


def _page_suffix(lft, tr):
    n = lft.shape[0]
    spec = pl.BlockSpec((tr, LANES), lambda i: (i, 0))
    return pl.pallas_call(
        _suffix_kernel,
        grid=(n // tr,),
        in_specs=[spec],
        out_specs=spec,
        out_shape=jax.ShapeDtypeStruct(lft.shape, F32),
        compiler_params=_cparams(("arbitrary",)),
        name="suffix",
    )(lft)


N_DECODE_SMALL = 9
N_DECODE_POOLS = 5


def _decode_parts(j, n_steps, refs, pages, lam_init):
    qd_ref, qf_ref, knd_ref, vnd_ref, knf_ref, vnf_ref, slogf_ref, lamv_ref, g_ref = refs[:N_DECODE_SMALL]
    rest = refs[N_DECODE_SMALL:]
    kd = rest[0:pages]
    vd = rest[pages:2 * pages]
    kf = rest[2 * pages:3 * pages]
    vf = rest[3 * pages:4 * pages]
    inc = rest[4 * pages:5 * pages]
    od_ref, of_ref = rest[N_DECODE_POOLS * pages], rest[N_DECODE_POOLS * pages + 1]
    md, ld, accd, mf, lf, accf, carry = rest[N_DECODE_POOLS * pages + 2:]
    rows = 2 * DIFF_HEADS
    th = DIFF_HEADS * LANES

    def init():
        @pl.when(j == 0)
        def _():
            md[...] = jnp.full(md.shape, NEG, F32)
            mf[...] = jnp.full(mf.shape, NEG, F32)
            ld[...] = jnp.zeros(ld.shape, F32)
            lf[...] = jnp.zeros(lf.shape, F32)
            accd[...] = jnp.zeros(accd.shape, F32)
            accf[...] = jnp.zeros(accf.shape, F32)
            carry[...] = jnp.zeros(carry.shape, F32)

    def queries():
        lane1 = lax.broadcasted_iota(jnp.int32, (rows, LANES), 1)
        r1 = lax.broadcasted_iota(jnp.int32, (rows, LANES), 0)
        qd = jnp.where((lane1 >> 6) == (r1 & 1), qd_ref[0], 0.0)
        col = lax.broadcasted_iota(jnp.int32, (rows, th), 1)
        r = lax.broadcasted_iota(jnp.int32, (rows, th), 0)
        own_head = (col & (DIFF_HEADS - 1)) == (r >> 1)
        half_mask = (col >> 6) == r
        qf = jnp.where(half_mask, qf_ref[0], 0.0)
        return qd, qf, own_head, half_mask, r1

    def update(m_ref, l_ref, acc_ref, s_list, pv_fn):
        s = jnp.concatenate(s_list, axis=1)
        w = s.shape[1] // pages
        m_old = m_ref[...]
        m_new = jnp.maximum(m_old, jnp.max(s, axis=1, keepdims=True))
        alpha = jnp.exp2(m_old - m_new)
        e = jnp.exp2(s - m_new)
        l_ref[...] = alpha * l_ref[...] + jnp.sum(e, axis=1, keepdims=True)
        pv = pv_fn(e[:, 0:w], 0)
        for k in range(1, pages):
            pv = pv + pv_fn(e[:, k * w:(k + 1) * w], k)
        acc_ref[...] = alpha * acc_ref[...] + pv
        m_ref[...] = m_new

    def main():
        qd, qf, own_head, _, _ = queries()
        sd = [jnp.where(own_head, _dot_nt(qd, kd[k][...]), NEG) for k in range(pages)]
        update(md, ld, accd, sd, lambda e, k: _dot(e, vd[k][...]))

        run = carry[...]
        base = slogf_ref[0]
        last_lane = lax.broadcasted_iota(jnp.int32, (rows, LANES), 1) == LANES - 1
        sf = [None] * pages
        for k in reversed(range(pages)):
            within = inc[k][...]
            later = jnp.where(last_lane, 0.0, pltpu.roll(within, LANES - 1, 1))
            sf[k] = _dot(qf, kf[k][...]) + LOG2E * ((base + run) + later)
            run = run + within[:, 0:1]
        carry[...] = run
        update(mf, lf, accf, sf, lambda e, k: _dot_nt(e, vf[k][...]))

    def final():
        @pl.when(j == n_steps - 1)
        def _():
            qd, qf, _, half_mask, r1 = queries()
            lam = _diff_lambda(lamv_ref[...], lam_init)

            def finish(m_ref, l_ref, acc_ref, q8, k_new, v_new):
                s_new = jnp.sum(q8 * k_new, axis=1, keepdims=True)
                m_old = m_ref[...]
                m_fin = jnp.maximum(m_old, s_new)
                a = jnp.exp2(m_old - m_fin)
                en = jnp.exp2(s_new - m_fin)
                l_fin = a * l_ref[...] + en
                return (a * acc_ref[...] + en * v_new) / l_fin

            od8 = finish(md, ld, accd, qd, knd_ref[0], vnd_ref[0])
            comb = od8 * jnp.where((r1 & 1) == 0, 1.0, -lam)
            o = comb + pltpu.roll(comb, rows - 1, 0)
            ms = jnp.mean(o * o, axis=1, keepdims=True)
            od_ref[0] = o * lax.rsqrt(ms + SUBLN_EPS) * g_ref[...] * (1.0 - lam_init)
            of8 = finish(mf, lf, accf, qf, knf_ref[0], vnf_ref[0])
            of_ref[0] = jnp.sum(jnp.where(half_mask, of8, 0.0), axis=0, keepdims=True)

    return init, main, final


def _decode_kernel(pt_ref, *refs, pages, lam_init):
    del pt_ref
    for part in _decode_parts(pl.program_id(1), pl.num_programs(1), refs, pages, lam_init):
        part()


def _decode_attention(page_table, dec_args, pages, lam_init):
    db, n_pages = page_table.shape
    n_steps = n_pages // pages
    rows = FOX_HEADS

    def page_map(p):
        return lambda b, j, pt: (pt[b, (n_steps - 1 - j) * pages + p], 0, 0)

    per_seq = lambda b, j, pt: (b, 0, 0)
    const = lambda b, j, pt: (0, 0)
    r128 = pl.BlockSpec((1, rows, LANES), per_seq)
    v512 = pl.BlockSpec((1, 1, WIDTH), per_seq)
    in_specs = [r128, v512, r128, r128, v512, v512,
                pl.BlockSpec((1, FOX_HEADS, 1), per_seq),
                pl.BlockSpec((4, HEAD_DIM), const),
                pl.BlockSpec((1, LANES), const)]
    assert len(in_specs) == N_DECODE_SMALL
    for _ in range(4):
        in_specs += [pl.BlockSpec((None, WIDTH, LANES), page_map(p)) for p in range(pages)]
    in_specs += [pl.BlockSpec((None, FOX_HEADS, LANES), page_map(p)) for p in range(pages)]
    assert len(dec_args) == N_DECODE_SMALL + N_DECODE_POOLS
    grid_spec = pltpu.PrefetchScalarGridSpec(
        num_scalar_prefetch=1,
        grid=(db, n_steps),
        in_specs=in_specs,
        out_specs=[r128, v512],
        scratch_shapes=[pltpu.VMEM((rows, 1), F32), pltpu.VMEM((rows, 1), F32), pltpu.VMEM((rows, LANES), F32),
                        pltpu.VMEM((rows, 1), F32), pltpu.VMEM((rows, 1), F32), pltpu.VMEM((rows, WIDTH), F32),
                        pltpu.VMEM((rows, LANES), F32)],
    )
    pools = dec_args[N_DECODE_SMALL:]
    args = list(dec_args[:N_DECODE_SMALL]) + [p for pool in pools for p in [pool] * pages]
    return pl.pallas_call(
        functools.partial(_decode_kernel, pages=pages, lam_init=lam_init),
        grid_spec=grid_spec,
        out_shape=[jax.ShapeDtypeStruct((db, rows, LANES), F32), jax.ShapeDtypeStruct((db, 1, WIDTH), F32)],
        compiler_params=_cparams(("arbitrary", "arbitrary")),
        name="decode",
    )(page_table, *args)


ROUTE_MEMBER = 3 * N_EXPERTS
GROUP_ROWS = 16


def _tail_kernel(mix_ref, x_ref, wo_ref, g_ref, b_ref, wr_ref, br_ref, h_ref, hb_ref, gate_ref, route_ref,
                 *, alpha, split):
    y = _dot(mix_ref[...].astype(BF16), wo_ref[...])
    h = _layer_norm(alpha * x_ref[...] + y, g_ref[...], b_ref[...])
    h_ref[...] = h
    hb = h.astype(BF16)
    hb_ref[...] = hb
    if split:
        two = _dot(hb, wr_ref[...])
        h_lo = (h - hb.astype(F32)).astype(BF16)
        logits = (two[:, :LANES] + two[:, LANES:]) + _dot(h_lo, wr_ref[:, :LANES]) + br_ref[...]
    else:
        logits = _dot_hi(h, wr_ref[...]) + br_ref[...]
    tm = logits.shape[0]
    lane = lax.broadcasted_iota(jnp.int32, (tm, LANES), 1).astype(F32)
    big = float(LANES)
    gmask = lane < N_GROUPS
    gl = jnp.where(gmask, logits, NEG)
    gmax = jnp.max(gl, axis=1, keepdims=True)
    gsum = jnp.sum(jnp.where(gmask, jnp.exp(gl - gmax), 0.0), axis=1, keepdims=True)
    g_val = 1.0 / gsum
    g_idx = jnp.min(jnp.where(gmask & (gl == gmax), lane, big), axis=1, keepdims=True)
    first = N_GROUPS + EXPERTS_PER_GROUP * g_idx
    emask = (lane >= first) & (lane < first + EXPERTS_PER_GROUP)
    el = jnp.where(emask, logits, NEG)
    v1 = jnp.max(el, axis=1, keepdims=True)
    i1 = jnp.min(jnp.where(emask & (el == v1), lane, big), axis=1, keepdims=True)
    el2 = jnp.where(lane == i1, NEG, el)
    v2 = jnp.max(el2, axis=1, keepdims=True)
    i2 = jnp.min(jnp.where(emask & (el2 == v2) & (lane != i1), lane, big), axis=1, keepdims=True)
    t = jnp.exp(v2 - v1)
    w1 = g_val / (1.0 + t)
    w2 = g_val * t / (1.0 + t)
    gates = jnp.where(lane == i1, w1, 0.0) + jnp.where(lane == i2, w2, 0.0)
    gate_ref[...] = gates
    hi = gates.astype(BF16).astype(F32)
    mid = (gates - hi).astype(BF16).astype(F32)
    lo = (gates - hi - mid).astype(BF16).astype(F32)
    route = (pltpu.roll(hi, LANES - N_GROUPS, 1) + pltpu.roll(mid, N_EXPERTS - N_GROUPS, 1)
             + pltpu.roll(lo, 2 * N_EXPERTS - N_GROUPS, 1))
    route = jnp.where(lane == ROUTE_MEMBER + g_idx, 1.0, route)
    route_ref[...] = route.astype(BF16)


def _tail(mix, x2d, wo_bf, g1, b1, wr, br_pad, tm, alpha):
    n = x2d.shape[0]
    row = lambda i: (i, 0)
    const = lambda i: (0, 0)
    return pl.pallas_call(
        functools.partial(_tail_kernel, alpha=alpha, split=wr.dtype == BF16),
        grid=(n // tm,),
        in_specs=[pl.BlockSpec((tm, D_MODEL), row), pl.BlockSpec((tm, D_MODEL), row),
                  pl.BlockSpec((D_MODEL, D_MODEL), const),
                  pl.BlockSpec((1, D_MODEL), const), pl.BlockSpec((1, D_MODEL), const),
                  pl.BlockSpec(wr.shape, const), pl.BlockSpec((1, LANES), const)],
        out_specs=[pl.BlockSpec((tm, D_MODEL), row), pl.BlockSpec((tm, D_MODEL), row),
                   pl.BlockSpec((tm, LANES), row), pl.BlockSpec((tm, LANES), row)],
        out_shape=[jax.ShapeDtypeStruct((n, D_MODEL), F32), jax.ShapeDtypeStruct((n, D_MODEL), BF16),
                   jax.ShapeDtypeStruct((n, LANES), F32), jax.ShapeDtypeStruct((n, LANES), BF16)],
        compiler_params=_cparams(("arbitrary",)),
        name="tail",
    )(mix, x2d, wo_bf, g1, b1, wr, br_pad)


def _swiglu_rows(x, gate, wg_ref, wu_ref, wd_ref):
    a = _dot(x, wg_ref[0])
    u = _dot(x, wu_ref[0])
    hmid = (a * jax.nn.sigmoid(a)) * u * gate
    return _dot(hmid.astype(BF16), wd_ref[0])


def _expert_specs(ix):
    return [pl.BlockSpec((1, D_MODEL, D_EXPERT), ix), pl.BlockSpec((1, D_MODEL, D_EXPERT), ix),
            pl.BlockSpec((1, D_EXPERT, D_MODEL), ix)]


def _moe_kernel(hb_ref, h_ref, gate_ref, wg_ref, wu_ref, wd_ref, g_ref, b_ref, y_ref, acc_ref, *, alpha):
    e = pl.program_id(1)

    @pl.when(e == 0)
    def _():
        acc_ref[...] = jnp.zeros(acc_ref.shape, F32)

    xb = hb_ref[...]
    lane = lax.broadcasted_iota(jnp.int32, (xb.shape[0], LANES), 1)
    gate = jnp.sum(jnp.where(lane == e + N_GROUPS, gate_ref[...], 0.0), axis=1, keepdims=True)
    acc_ref[...] += _swiglu_rows(xb, gate, wg_ref, wu_ref, wd_ref)

    @pl.when(e == N_EXPERTS - 1)
    def _():
        y_ref[...] = _layer_norm(alpha * h_ref[...] + acc_ref[...], g_ref[...], b_ref[...])


def _moe(hb, h, gates, wg_bf, wu_bf, wd_bf, g2, b2, tm, alpha):
    n = h.shape[0]
    row = lambda i, e: (i, 0)
    const = lambda i, e: (0, 0)
    return pl.pallas_call(
        functools.partial(_moe_kernel, alpha=alpha),
        grid=(n // tm, N_EXPERTS),
        in_specs=[pl.BlockSpec((tm, D_MODEL), row), pl.BlockSpec((tm, D_MODEL), row),
                  pl.BlockSpec((tm, LANES), row)] + _expert_specs(lambda i, e: (e, 0, 0))
                 + [pl.BlockSpec((1, D_MODEL), const), pl.BlockSpec((1, D_MODEL), const)],
        out_specs=pl.BlockSpec((tm, D_MODEL), row),
        out_shape=jax.ShapeDtypeStruct((n, D_MODEL), F32),
        scratch_shapes=[pltpu.VMEM((tm, D_MODEL), F32)],
        compiler_params=_cparams(("arbitrary", "arbitrary")),
        name="moe",
    )(hb, h, gates, wg_bf, wu_bf, wd_bf, g2, b2)


def _route_gate(route, lane, e):
    pick = (lane == e) | (lane == e + N_EXPERTS) | (lane == e + 2 * N_EXPERTS)
    return jnp.sum(jnp.where(pick, route, 0.0), axis=1, keepdims=True)


def _moe_grouped_kernel(hb_ref, h_ref, route_ref, wg_ref, wu_ref, wd_ref, g_ref, b_ref, y_ref,
                        acc_ref, rcol_ref, rrow_ref, mrow_ref, xg_ref, gsel_ref, yg_ref, cnt_ref,
                        *, alpha, cap):
    e = pl.program_id(1)
    grp = e // EXPERTS_PER_GROUP
    ts = hb_ref.shape[0]
    cap_pad = yg_ref.shape[0]
    lane = lax.broadcasted_iota(jnp.int32, (ts, LANES), 1)

    @pl.when(e == 0)
    def _():
        acc_ref[...] = jnp.zeros(acc_ref.shape, F32)
        t = lax.broadcasted_iota(jnp.int32, (LANES, LANES), 0)
        u = lax.broadcasted_iota(jnp.int32, (LANES, LANES), 1)
        lower = (u < t).astype(BF16)
        carry_c = jnp.zeros((1, LANES), F32)
        carry_r = jnp.zeros((GROUP_ROWS, 1), F32)
        for blk in range(ts // LANES):
            sl = slice(blk * LANES, (blk + 1) * LANES)
            memb = jnp.where((u >= ROUTE_MEMBER) & (u < ROUTE_MEMBER + N_GROUPS),
                             route_ref[sl, :].astype(F32), 0.0)
            rcol_ref[sl, :] = _dot(lower, memb.astype(BF16)) + carry_c
            carry_c = carry_c + jnp.sum(memb, axis=0, keepdims=True)
            mrow = memb.T[ROUTE_MEMBER:ROUTE_MEMBER + GROUP_ROWS, :]
            mrow_ref[:, sl] = mrow
            rrow_ref[:, sl] = _dot_nt(mrow.astype(BF16), lower) + carry_r
            carry_r = carry_r + jnp.sum(mrow, axis=1, keepdims=True)
        lane1 = lax.broadcasted_iota(jnp.int32, (1, LANES), 1)
        for gi in range(N_GROUPS):
            cnt_ref[gi] = jnp.sum(jnp.where(lane1 == ROUTE_MEMBER + gi, carry_c, 0.0)).astype(jnp.int32)

    small = cnt_ref[grp] <= cap

    @pl.when((e % EXPERTS_PER_GROUP == 0) & small)
    def _():
        pos = lax.broadcasted_iota(jnp.int32, (cap, ts), 0).astype(F32)
        rank = jnp.where(mrow_ref[pl.ds(grp, 1), :] > 0.5, rrow_ref[pl.ds(grp, 1), :], -1.0)
        onehot = jnp.where(rank == pos, 1.0, 0.0).astype(BF16)
        xg_ref[...] = _dot(onehot, hb_ref[...]).astype(BF16)
        gsel_ref[...] = _dot(onehot, route_ref[...])
        yg_ref[...] = jnp.zeros(yg_ref.shape, F32)

    @pl.when(small)
    def _():
        lane_c = lax.broadcasted_iota(jnp.int32, (cap, LANES), 1)
        gate = _route_gate(gsel_ref[...], lane_c, e)
        yg_ref[0:cap, :] += _swiglu_rows(xg_ref[...], gate, wg_ref, wu_ref, wd_ref)

    @pl.when(jnp.logical_not(small))
    def _():
        gate = _route_gate(route_ref[...].astype(F32), lane, e)
        acc_ref[...] += _swiglu_rows(hb_ref[...], gate, wg_ref, wu_ref, wd_ref)

    @pl.when((e % EXPERTS_PER_GROUP == EXPERTS_PER_GROUP - 1) & small)
    def _():
        mine = lane == ROUTE_MEMBER + grp
        rank = jnp.sum(jnp.where(mine, rcol_ref[...], 0.0), axis=1, keepdims=True)
        member = jnp.sum(jnp.where(mine, route_ref[...].astype(F32), 0.0), axis=1, keepdims=True)
        pos = lax.broadcasted_iota(jnp.int32, (ts, cap_pad), 1).astype(F32)
        rank = jnp.where(member > 0.5, rank, -1.0)
        onehot_t = jnp.where(rank == pos, 1.0, 0.0).astype(BF16)
        acc_ref[...] += _dot(onehot_t, yg_ref[...].astype(BF16))

    @pl.when(e == N_EXPERTS - 1)
    def _():
        y_ref[...] = _layer_norm(alpha * h_ref[...] + acc_ref[...], g_ref[...], b_ref[...])


def _moe_grouped(hb, h, route, wg_bf, wu_bf, wd_bf, g2, b2, ts, cap, alpha):
    n = h.shape[0]
    cap_pad = -(-cap // LANES) * LANES
    row = lambda i, e: (i, 0)
    const = lambda i, e: (0, 0)
    return pl.pallas_call(
        functools.partial(_moe_grouped_kernel, alpha=alpha, cap=cap),
        grid=(n // ts, N_EXPERTS),
        in_specs=[pl.BlockSpec((ts, D_MODEL), row), pl.BlockSpec((ts, D_MODEL), row),
                  pl.BlockSpec((ts, LANES), row)] + _expert_specs(lambda i, e: (e, 0, 0))
                 + [pl.BlockSpec((1, D_MODEL), const), pl.BlockSpec((1, D_MODEL), const)],
        out_specs=pl.BlockSpec((ts, D_MODEL), row),
        out_shape=jax.ShapeDtypeStruct((n, D_MODEL), F32),
        scratch_shapes=[pltpu.VMEM((ts, D_MODEL), F32),
                        pltpu.VMEM((ts, LANES), F32),
                        pltpu.VMEM((GROUP_ROWS, ts), F32),
                        pltpu.VMEM((GROUP_ROWS, ts), F32),
                        pltpu.VMEM((cap, D_MODEL), BF16),
                        pltpu.VMEM((cap, LANES), F32),
                        pltpu.VMEM((cap_pad, D_MODEL), F32),
                        pltpu.SMEM((N_GROUPS,), jnp.int32)],
        compiler_params=_cparams(("arbitrary", "arbitrary")),
        name="moe_grouped",
    )(hb, h, route, wg_bf, wu_bf, wd_bf, g2, b2)


def _rope_tables(pos):
    half = HEAD_DIM // 2
    inv = ROPE_THETA ** (-jnp.arange(half, dtype=F32) * 2.0 / HEAD_DIM)
    ang = pos.astype(F32)[:, None] * inv[None, :]
    cos, sin = jnp.cos(ang), jnp.sin(ang)
    reps = WIDTH // HEAD_DIM
    cos_t = jnp.tile(jnp.concatenate([cos, cos], axis=1), (1, reps))
    sin_t = jnp.tile(jnp.concatenate([-sin, sin], axis=1), (1, reps))
    return cos_t, sin_t


def kernel(x_prompt, x_sample, cache_diff_k, cache_diff_v, cache_fox_k, cache_fox_v, cache_fox_logf, page_table,
           w_in, b_forget, lambda_q1, lambda_k1, lambda_q2, lambda_k2, subln_g, w_out, ln1_g, ln1_b,
           w_router_group, b_router_group, w_router_expert, b_router_expert, w_gate, w_up, w_down, ln2_g, ln2_b):
    B, S, D = x_prompt.shape
    DB, T, _ = x_sample.shape
    depth = w_in.shape[0]
    assert depth == 1 and T == 1 and D == D_MODEL
    n_pool, page = cache_diff_k.shape[1], cache_diff_k.shape[2]
    n_pages = page_table.shape[1]
    past_len = n_pages * page
    assert page == LANES
    alpha = (2 * depth) ** 0.25
    l = 0
    lam_init = 0.8 - 0.6 * math.exp(-0.3 * l)

    w_pad = jnp.pad(w_in[l], ((0, 0), (0, 6 * WIDTH + LANES - w_in.shape[2]))).astype(BF16)
    bf_pad = jnp.pad(b_forget[l], (0, LANES - FOX_HEADS)).reshape(1, LANES)
    lamv = jnp.stack([lambda_q1[l], lambda_k1[l], lambda_q2[l], lambda_k2[l]])
    g1 = subln_g[l].reshape(1, LANES)
    wo_bf = w_out[l].astype(BF16)
    wr_pad = jnp.pad(jnp.concatenate([w_router_group[l], w_router_expert[l]], axis=1),
                     ((0, 0), (0, LANES - N_GROUPS - N_EXPERTS)))
    br_pad = jnp.pad(jnp.concatenate([b_router_group[l], b_router_expert[l]]),
                     (0, LANES - N_GROUPS - N_EXPERTS)).reshape(1, LANES)
    wg_bf, wu_bf, wd_bf = w_gate[l].astype(BF16), w_up[l].astype(BF16), w_down[l].astype(BF16)
    ln1g, ln1b = ln1_g[l].reshape(1, D), ln1_b[l].reshape(1, D)
    ln2g, ln2b = ln2_g[l].reshape(1, D), ln2_b[l].reshape(1, D)
    p_cos, p_sin = _rope_tables(jnp.arange(S, dtype=jnp.int32))
    s_cos, s_sin = _rope_tables(jnp.full((DB,), past_len, dtype=jnp.int32))

    tq, tp = ATTN_TILE, PROJ_TILE
    xp2 = x_prompt.reshape(B * S, D)
    (q_p, k_p, dk, dv, fk, fv, logf, vt_p) = _project(xp2, w_pad, p_cos, p_sin, bf_pad, tp, S // tp, True)
    logf_t = jnp.swapaxes(logf.reshape(B, S, FOX_HEADS), 1, 2).reshape(B * FOX_HEADS, S)
    c_t = _cumsum_rows(logf_t).reshape(B, FOX_HEADS, S)
    c = jnp.swapaxes(c_t, 1, 2)
    c_blk = jnp.swapaxes(c_t.reshape(B, FOX_HEADS, S // tq, tq), 1, 2)
    mix_p = _prompt_attention(q_p.reshape(B, S, 2 * WIDTH), k_p.reshape(B, S, 2 * WIDTH),
                              vt_p.reshape(B, S // tp, 2 * WIDTH, tp), c, c_blk, lamv, g1, tq, lam_init)
    wr_hi = wr_pad.astype(BF16)
    wr_split = jnp.concatenate([wr_hi, (wr_pad - wr_hi.astype(F32)).astype(BF16)], axis=1)
    h_p, hb_p, _, route_p = _tail(mix_p.reshape(B * S, D), xp2, wo_bf, ln1g, ln1b, wr_split, br_pad, TAIL_TILE, alpha)
    y_p = _moe_grouped(hb_p, h_p, route_p, wg_bf, wu_bf, wd_bf, ln2g, ln2b, MOE_TILE, MOE_GROUP_CAP, alpha)

    xs2 = x_sample.reshape(DB, D)
    (q_s, _, sdk, sdv, sfk, sfv, slogf) = _project(xs2, w_pad, s_cos, s_sin, bf_pad, DB, 1, False)
    sdq, sfq = q_s[:, :WIDTH], q_s[:, WIDTH:]
    diff_pool = lambda a: a[l].reshape(n_pool, page * DIFF_HEADS, 2 * HEAD_DIM)
    fox_pool = lambda a: jnp.transpose(a[l], (0, 2, 3, 1)).reshape(n_pool, FOX_HEADS * HEAD_DIM, page)
    lft = jnp.transpose(cache_fox_logf[l], (0, 2, 1)).reshape(n_pool * FOX_HEADS, page)
    suffix = _page_suffix(lft, 4096)
    rep = lambda a: jnp.repeat(a.astype(F32).reshape(DB, DIFF_HEADS, 2 * HEAD_DIM), 2, axis=1)
    v3 = lambda a: a.astype(F32).reshape(DB, 1, WIDTH)
    dec_args = [rep(sdq), v3(sfq), rep(sdk), rep(sdv), v3(sfk), v3(sfv), slogf.reshape(DB, FOX_HEADS, 1), lamv, g1,
                diff_pool(cache_diff_k), diff_pool(cache_diff_v), fox_pool(cache_fox_k), fox_pool(cache_fox_v),
                suffix.reshape(n_pool, FOX_HEADS, page)]
    od_s, of_s = _decode_attention(page_table, dec_args, DECODE_PAGES, lam_init)
    mix_s = jnp.concatenate([od_s[:, 0::2, :].reshape(DB, WIDTH), of_s.reshape(DB, WIDTH)], axis=1)
    h_s, hb_s, gates_s, _ = _tail(mix_s, xs2, wo_bf, ln1g, ln1b, wr_pad, br_pad, DB, alpha)
    y_s = _moe(hb_s, h_s, gates_s, wg_bf, wu_bf, wd_bf, ln2g, ln2b, DB, alpha)

    fox_out = lambda a: jnp.transpose(a.reshape(B, FOX_HEADS, HEAD_DIM, S), (0, 3, 1, 2))[None]
    return (y_p.reshape(B, S, D), y_s.reshape(DB, T, D),
            dk.reshape(1, B, S, DIFF_HEADS, 2 * HEAD_DIM), dv.reshape(1, B, S, DIFF_HEADS, 2 * HEAD_DIM),
            fox_out(fk), fox_out(fv),
            logf.reshape(1, B, S, FOX_HEADS),
            sdk.reshape(1, DB, T, DIFF_HEADS, 2 * HEAD_DIM), sdv.reshape(1, DB, T, DIFF_HEADS, 2 * HEAD_DIM),
            sfk.reshape(1, DB, T, FOX_HEADS, HEAD_DIM), sfv.reshape(1, DB, T, FOX_HEADS, HEAD_DIM),
            slogf.reshape(1, DB, T, FOX_HEADS))
```

```python
import functools
import math

import jax
import jax.numpy as jnp
from jax import lax
from jax.experimental import pallas as pl
from jax.experimental.pallas import tpu as pltpu

F32 = jnp.float32
BF16 = jnp.bfloat16

HEAD_DIM = 64
DIFF_HEADS = 4
FOX_HEADS = 8
WIDTH = 512
D_MODEL = 1024
N_GROUPS = 4
EXPERTS_PER_GROUP = 4
N_EXPERTS = 16
D_EXPERT = 512
ROPE_THETA = 10000.0
LN_EPS = 1e-5
SUBLN_EPS = 1e-5
LANES = 128
NEG = -1e30
LOG2E = 1.4426950408889634
V7X_VMEM_LIMIT = 48 * 1024 * 1024
ATTN_TILE = 512
PROJ_TILE = 512
DECODE_PAGES = 16
TAIL_TILE = 512
MOE_TILE = 1024
MOE_GROUP_CAP = 320


def _cparams(sem):
    return pltpu.CompilerParams(dimension_semantics=sem, vmem_limit_bytes=V7X_VMEM_LIMIT)


def _dot(a, b):
    return jnp.dot(a, b, preferred_element_type=F32)


def _dot_nt(a, b):
    return lax.dot_general(a, b, (((1,), (1,)), ((), ())), preferred_element_type=F32)


def _dot_hi(a, b):
    return jnp.dot(a, b, preferred_element_type=F32, precision=lax.Precision.HIGHEST)


def _layer_norm(x, g, b):
    mu = jnp.mean(x, axis=-1, keepdims=True)
    xc = x - mu
    var = jnp.mean(xc * xc, axis=-1, keepdims=True)
    return xc * lax.rsqrt(var + LN_EPS) * g + b


def _diff_lambda(lv, lam_init):
    return (jnp.exp(jnp.sum(lv[0:1] * lv[1:2], axis=1, keepdims=True))
            - jnp.exp(jnp.sum(lv[2:3] * lv[3:4], axis=1, keepdims=True)) + lam_init)


def _proj_kernel(x_ref, w_ref, cos_ref, sin_ref, bf_ref,
                 q_ref, kb_ref, dk_ref, dv_ref, fk_ref, fv_ref, logf_ref, *vt_refs):
    tm = x_ref.shape[0]
    xb = x_ref[...].astype(BF16)
    cos = cos_ref[...]
    sin = sin_ref[...]
    lane = lax.broadcasted_iota(jnp.int32, (tm, WIDTH), 1)
    first_half = (lane & (HEAD_DIM // 2)) == 0
    qscale = HEAD_DIM ** -0.5 * LOG2E

    def mm(c):
        return _dot(xb, w_ref[:, c * WIDTH:(c + 1) * WIDTH])

    def rope(p):
        partner = jnp.where(first_half,
                            pltpu.roll(p, WIDTH - HEAD_DIM // 2, 1),
                            pltpu.roll(p, HEAD_DIM // 2, 1))
        return p * cos + partner * sin

    dq = rope(mm(0))
    q_ref[:, :WIDTH] = (dq * qscale).astype(BF16)
    dk = rope(mm(1))
    for hh in range(DIFF_HEADS):
        dk_ref[:, hh, :] = dk[:, hh * LANES:(hh + 1) * LANES]
    kb_ref[:, :WIDTH] = dk.astype(BF16)
    dv = mm(2)
    for hh in range(DIFF_HEADS):
        dv_ref[:, hh, :] = dv[:, hh * LANES:(hh + 1) * LANES]
    q_ref[:, WIDTH:] = (mm(3) * qscale).astype(BF16)
    fk = mm(4)
    kb_ref[:, WIDTH:] = fk.astype(BF16)
    fv = mm(5)
    if vt_refs:
        fvt = fv.T
        fk_ref[0] = fk.T
        fv_ref[0] = fvt
        vt_refs[0][0, :WIDTH, :] = dv.T.astype(BF16)
        vt_refs[0][0, WIDTH:, :] = fvt.astype(BF16)
    else:
        fk_ref[...] = fk
        fv_ref[...] = fv
    z = _dot(xb, w_ref[:, 6 * WIDTH:6 * WIDTH + LANES]) + bf_ref[...]
    logf = jnp.minimum(z, 0.0) - jnp.log1p(jnp.exp(-jnp.abs(z)))
    logf_ref[...] = logf[:, :FOX_HEADS]


def _project(x2d, w_bf, cos_t, sin_t, bf_pad, tm, table_blocks, prompt_layout):
    n = x2d.shape[0]
    grid = (n // tm,)
    row = lambda i: (i, 0)
    tab = lambda i: (i % table_blocks, 0)
    const = lambda i: (0, 0)
    wide = pl.BlockSpec((tm, WIDTH), row)
    both = pl.BlockSpec((tm, 2 * WIDTH), row)
    out_shape = ([jax.ShapeDtypeStruct((n, 2 * WIDTH), BF16)] * 2
                 + [jax.ShapeDtypeStruct((n, DIFF_HEADS, LANES), F32)] * 2
                 + [jax.ShapeDtypeStruct((n, WIDTH), F32)] * 2
                 + [jax.ShapeDtypeStruct((n, FOX_HEADS), F32)])
    head3 = pl.BlockSpec((tm, DIFF_HEADS, LANES), lambda i: (i, 0, 0))
    out_specs = [both, both, head3, head3, wide, wide, pl.BlockSpec((tm, FOX_HEADS), row)]
    if prompt_layout:
        n_seq = n // (table_blocks * tm)
        by_seq = pl.BlockSpec((1, WIDTH, tm), lambda i: (i // table_blocks, 0, i % table_blocks))
        out_shape[4:6] = [jax.ShapeDtypeStruct((n_seq, WIDTH, table_blocks * tm), F32)] * 2
        out_specs[4:6] = [by_seq, by_seq]
        out_shape += [jax.ShapeDtypeStruct((n // tm, 2 * WIDTH, tm), BF16)]
        out_specs += [pl.BlockSpec((1, 2 * WIDTH, tm), lambda i: (i, 0, 0))]
    return pl.pallas_call(
        _proj_kernel,
        grid=grid,
        in_specs=[pl.BlockSpec((tm, D_MODEL), row),
                  pl.BlockSpec(w_bf.shape, const),
                  pl.BlockSpec((tm, WIDTH), tab),
                  pl.BlockSpec((tm, WIDTH), tab),
                  pl.BlockSpec((1, LANES), const)],
        out_specs=out_specs,
        out_shape=out_shape,
        compiler_params=_cparams(("arbitrary",)),
        name="proj",
    )(x2d, w_bf, cos_t, sin_t, bf_pad)


def _cumsum_kernel(x_ref, o_ref):
    rows, s = x_ref.shape
    t = lax.broadcasted_iota(jnp.int32, (LANES, LANES), 0)
    u = lax.broadcasted_iota(jnp.int32, (LANES, LANES), 1)
    tri = (t <= u).astype(F32)
    carry = jnp.zeros((rows, 1), F32)
    for ch in range(s // LANES):
        blk = _dot_hi(x_ref[:, ch * LANES:(ch + 1) * LANES], tri) + carry
        o_ref[:, ch * LANES:(ch + 1) * LANES] = blk * LOG2E
        carry = blk[:, LANES - 1:LANES]


def _cumsum_rows(xt):
    return pl.pallas_call(
        _cumsum_kernel,
        out_shape=jax.ShapeDtypeStruct(xt.shape, F32),
        name="cumsum",
    )(xt)


def _attn_kernel(dq_ref, fq_ref, dk_ref, fk_ref, dvt_ref, fvt_ref, c_ref, cq_ref, lamv_ref, g_ref,
                 mix_ref, *, tq, lam_init):
    qi = pl.program_id(1)
    tk = tq
    lane = lax.broadcasted_iota(jnp.int32, (tq, LANES), 1)
    lo = lane < HEAD_DIM
    vb = dvt_ref.shape[3]
    causal = lax.broadcasted_iota(jnp.int32, (tk, tq), 0) <= lax.broadcasted_iota(jnp.int32, (tk, tq), 1)
    top = lax.broadcasted_iota(jnp.int32, (LANES, tq), 0) < HEAD_DIM
    lam = _diff_lambda(lamv_ref[...], lam_init)
    g = g_ref[...]

    for u in range(DIFF_HEADS + FOX_HEADS // 2):
        is_diff = u < DIFF_HEADS
        off = (u if is_diff else u - DIFF_HEADS) * LANES
        k_ref, vt_ref = (dk_ref, dvt_ref) if is_diff else (fk_ref, fvt_ref)
        q_pair = (dq_ref if is_diff else fq_ref)[0, :, off:off + LANES]
        zero = jnp.zeros_like(q_pair)
        qs = (jnp.where(lo, q_pair, zero), jnp.where(lo, zero, q_pair))
        h0 = None if is_diff else 2 * (u - DIFF_HEADS)

        def block(j, carry, masked, off=off, k_ref=k_ref, vt_ref=vt_ref, qs=qs, h0=h0):
            start = pl.multiple_of(j * tk, tk)
            k = k_ref[0, pl.ds(start, tk), off:off + LANES]
            scores = [_dot_nt(k, q) for q in qs]
            out = []
            for idx in range(2):
                m, l, acc = carry[idx]
                st = scores[idx]
                if h0 is None:
                    cq = None
                else:
                    st = st - c_ref[0, pl.ds(start, tk), h0 + idx:h0 + idx + 1]
                    cq = cq_ref[0, 0, h0 + idx:h0 + idx + 1, :]
                if masked:
                    st = jnp.where(causal, st, NEG)
                mt = jnp.max(st, axis=0, keepdims=True)
                m_new = jnp.maximum(m, mt if cq is None else mt + cq)
                alpha = jnp.exp2(m - m_new)
                e = jnp.exp2(st - (m_new if cq is None else m_new - cq))
                l = alpha * l + jnp.sum(e, axis=0, keepdims=True)
                e = e.astype(BF16)
                acc = alpha * acc
                for i in range(tk // vb):
                    acc = acc + _dot(vt_ref[0, j * (tk // vb) + i, off:off + LANES, :], e[i * vb:(i + 1) * vb])
                out.append((m_new, l, acc))
            return tuple(out)

        init = tuple((jnp.full((1, tq), NEG, F32), jnp.zeros((1, tq), F32), jnp.zeros((LANES, tq), F32))
                     for _ in range(2))
        carry = lax.fori_loop(0, qi, lambda j, c: block(j, c, False), init)
        (_, la, acca), (_, lb, accb) = block(qi, carry, True)
        if is_diff:
            o = (acca / la - lam * (accb / lb)).T
            ms = jnp.mean(o * o, axis=1, keepdims=True)
            o = o * lax.rsqrt(ms + SUBLN_EPS) * g * (1.0 - lam_init)
            mix_ref[0, :, off:off + LANES] = o.astype(BF16)
        else:
            o = jnp.where(top, acca / la, accb / lb).T
            mix_ref[0, :, WIDTH + off:WIDTH + off + LANES] = o.astype(BF16)


def _prompt_attention(q, k, vt, c, cq, lamv, g, tq, lam_init):
    b, s, _ = q.shape
    n_vb, vb = vt.shape[1], vt.shape[3]
    assert tq % vb == 0
    qspec = lambda half: pl.BlockSpec((1, tq, WIDTH), lambda bi, qi: (bi, qi, half))
    kspec = lambda half: pl.BlockSpec((1, s, WIDTH), lambda bi, qi: (bi, 0, half))
    vspec = lambda half: pl.BlockSpec((1, n_vb, WIDTH, vb), lambda bi, qi: (bi, 0, half, 0))
    return pl.pallas_call(
        functools.partial(_attn_kernel, tq=tq, lam_init=lam_init),
        grid=(b, s // tq),
        in_specs=[qspec(0), qspec(1), kspec(0), kspec(1), vspec(0), vspec(1),
                  pl.BlockSpec((1, s, FOX_HEADS), lambda bi, qi: (bi, 0, 0)),
                  pl.BlockSpec((1, 1, FOX_HEADS, tq), lambda bi, qi: (bi, qi, 0, 0)),
                  pl.BlockSpec((4, HEAD_DIM), lambda bi, qi: (0, 0)),
                  pl.BlockSpec((1, LANES), lambda bi, qi: (0, 0))],
        out_specs=pl.BlockSpec((1, tq, D_MODEL), lambda bi, qi: (bi, qi, 0)),
        out_shape=jax.ShapeDtypeStruct((b, s, D_MODEL), BF16),
        compiler_params=_cparams(("arbitrary", "arbitrary")),
        name="attn",
    )(q, q, k, k, vt, vt, c, cq, lamv, g)


def _suffix_kernel(x_ref, s_ref):
    t = lax.broadcasted_iota(jnp.int32, (LANES, LANES), 0)
    u = lax.broadcasted_iota(jnp.int32, (LANES, LANES), 1)
    s_ref[...] = _dot_hi(x_ref[...], (t >= u).astype(F32))


def _page_suffix(lft, tr):
    n = lft.shape[0]
    spec = pl.BlockSpec((tr, LANES), lambda i: (i, 0))
    return pl.pallas_call(
        _suffix_kernel,
        grid=(n // tr,),
        in_specs=[spec],
        out_specs=spec,
        out_shape=jax.ShapeDtypeStruct(lft.shape, F32),
        compiler_params=_cparams(("arbitrary",)),
        name="suffix",
    )(lft)


N_DECODE_SMALL = 9
N_DECODE_POOLS = 5


def _decode_parts(j, n_steps, refs, pages, lam_init):
    qd_ref, qf_ref, knd_ref, vnd_ref, knf_ref, vnf_ref, slogf_ref, lamv_ref, g_ref = refs[:N_DECODE_SMALL]
    rest = refs[N_DECODE_SMALL:]
    kd = rest[0:pages]
    vd = rest[pages:2 * pages]
    kf = rest[2 * pages:3 * pages]
    vf = rest[3 * pages:4 * pages]
    inc = rest[4 * pages:5 * pages]
    od_ref, of_ref = rest[N_DECODE_POOLS * pages], rest[N_DECODE_POOLS * pages + 1]
    md, ld, accd, mf, lf, accf, carry = rest[N_DECODE_POOLS * pages + 2:]
    rows = 2 * DIFF_HEADS
    th = DIFF_HEADS * LANES

    def init():
        @pl.when(j == 0)
        def _():
            md[...] = jnp.full(md.shape, NEG, F32)
            mf[...] = jnp.full(mf.shape, NEG, F32)
            ld[...] = jnp.zeros(ld.shape, F32)
            lf[...] = jnp.zeros(lf.shape, F32)
            accd[...] = jnp.zeros(accd.shape, F32)
            accf[...] = jnp.zeros(accf.shape, F32)
            carry[...] = jnp.zeros(carry.shape, F32)

    def queries():
        lane1 = lax.broadcasted_iota(jnp.int32, (rows, LANES), 1)
        r1 = lax.broadcasted_iota(jnp.int32, (rows, LANES), 0)
        qd = jnp.where((lane1 >> 6) == (r1 & 1), qd_ref[0], 0.0)
        col = lax.broadcasted_iota(jnp.int32, (rows, th), 1)
        r = lax.broadcasted_iota(jnp.int32, (rows, th), 0)
        own_head = (col & (DIFF_HEADS - 1)) == (r >> 1)
        half_mask = (col >> 6) == r
        qf = jnp.where(half_mask, qf_ref[0], 0.0)
        return qd, qf, own_head, half_mask, r1

    def update(m_ref, l_ref, acc_ref, s_list, pv_fn):
        s = jnp.concatenate(s_list, axis=1)
        w = s.shape[1] // pages
        m_old = m_ref[...]
        m_new = jnp.maximum(m_old, jnp.max(s, axis=1, keepdims=True))
        alpha = jnp.exp2(m_old - m_new)
        e = jnp.exp2(s - m_new)
        l_ref[...] = alpha * l_ref[...] + jnp.sum(e, axis=1, keepdims=True)
        pv = pv_fn(e[:, 0:w], 0)
        for k in range(1, pages):
            pv = pv + pv_fn(e[:, k * w:(k + 1) * w], k)
        acc_ref[...] = alpha * acc_ref[...] + pv
        m_ref[...] = m_new

    def main():
        qd, qf, own_head, _, _ = queries()
        sd = [jnp.where(own_head, _dot_nt(qd, kd[k][...]), NEG) for k in range(pages)]
        update(md, ld, accd, sd, lambda e, k: _dot(e, vd[k][...]))

        run = carry[...]
        base = slogf_ref[0]
        last_lane = lax.broadcasted_iota(jnp.int32, (rows, LANES), 1) == LANES - 1
        sf = [None] * pages
        for k in reversed(range(pages)):
            within = inc[k][...]
            later = jnp.where(last_lane, 0.0, pltpu.roll(within, LANES - 1, 1))
            sf[k] = _dot(qf, kf[k][...]) + LOG2E * ((base + run) + later)
            run = run + within[:, 0:1]
        carry[...] = run
        update(mf, lf, accf, sf, lambda e, k: _dot_nt(e, vf[k][...]))

    def final():
        @pl.when(j == n_steps - 1)
        def _():
            qd, qf, _, half_mask, r1 = queries()
            lam = _diff_lambda(lamv_ref[...], lam_init)

            def finish(m_ref, l_ref, acc_ref, q8, k_new, v_new):
                s_new = jnp.sum(q8 * k_new, axis=1, keepdims=True)
                m_old = m_ref[...]
                m_fin = jnp.maximum(m_old, s_new)
                a = jnp.exp2(m_old - m_fin)
                en = jnp.exp2(s_new - m_fin)
                l_fin = a * l_ref[...] + en
                return (a * acc_ref[...] + en * v_new) / l_fin

            od8 = finish(md, ld, accd, qd, knd_ref[0], vnd_ref[0])
            comb = od8 * jnp.where((r1 & 1) == 0, 1.0, -lam)
            o = comb + pltpu.roll(comb, rows - 1, 0)
            ms = jnp.mean(o * o, axis=1, keepdims=True)
            od_ref[0] = o * lax.rsqrt(ms + SUBLN_EPS) * g_ref[...] * (1.0 - lam_init)
            of8 = finish(mf, lf, accf, qf, knf_ref[0], vnf_ref[0])
            of_ref[0] = jnp.sum(jnp.where(half_mask, of8, 0.0), axis=0, keepdims=True)

    return init, main, final


def _decode_kernel(pt_ref, *refs, pages, lam_init):
    del pt_ref
    for part in _decode_parts(pl.program_id(1), pl.num_programs(1), refs, pages, lam_init):
        part()


def _decode_attention(page_table, dec_args, pages, lam_init):
    db, n_pages = page_table.shape
    n_steps = n_pages // pages
    rows = FOX_HEADS

    def page_map(p):
        return lambda b, j, pt: (pt[b, (n_steps - 1 - j) * pages + p], 0, 0)

    per_seq = lambda b, j, pt: (b, 0, 0)
    const = lambda b, j, pt: (0, 0)
    r128 = pl.BlockSpec((1, rows, LANES), per_seq)
    v512 = pl.BlockSpec((1, 1, WIDTH), per_seq)
    in_specs = [r128, v512, r128, r128, v512, v512,
                pl.BlockSpec((1, FOX_HEADS, 1), per_seq),
                pl.BlockSpec((4, HEAD_DIM), const),
                pl.BlockSpec((1, LANES), const)]
    assert len(in_specs) == N_DECODE_SMALL
    for _ in range(4):
        in_specs += [pl.BlockSpec((None, WIDTH, LANES), page_map(p)) for p in range(pages)]
    in_specs += [pl.BlockSpec((None, FOX_HEADS, LANES), page_map(p)) for p in range(pages)]
    assert len(dec_args) == N_DECODE_SMALL + N_DECODE_POOLS
    grid_spec = pltpu.PrefetchScalarGridSpec(
        num_scalar_prefetch=1,
        grid=(db, n_steps),
        in_specs=in_specs,
        out_specs=[r128, v512],
        scratch_shapes=[pltpu.VMEM((rows, 1), F32), pltpu.VMEM((rows, 1), F32), pltpu.VMEM((rows, LANES), F32),
                        pltpu.VMEM((rows, 1), F32), pltpu.VMEM((rows, 1), F32), pltpu.VMEM((rows, WIDTH), F32),
                        pltpu.VMEM((rows, LANES), F32)],
    )
    pools = dec_args[N_DECODE_SMALL:]
    args = list(dec_args[:N_DECODE_SMALL]) + [p for pool in pools for p in [pool] * pages]
    return pl.pallas_call(
        functools.partial(_decode_kernel, pages=pages, lam_init=lam_init),
        grid_spec=grid_spec,
        out_shape=[jax.ShapeDtypeStruct((db, rows, LANES), F32), jax.ShapeDtypeStruct((db, 1, WIDTH), F32)],
        compiler_params=_cparams(("arbitrary", "arbitrary")),
        name="decode",
    )(page_table, *args)


ROUTE_MEMBER = 3 * N_EXPERTS
GROUP_ROWS = 16


def _tail_kernel(mix_ref, x_ref, wo_ref, g_ref, b_ref, wr_ref, br_ref, h_ref, hb_ref, gate_ref, route_ref,
                 *, alpha, split):
    y = _dot(mix_ref[...].astype(BF16), wo_ref[...])
    h = _layer_norm(alpha * x_ref[...] + y, g_ref[...], b_ref[...])
    h_ref[...] = h
    hb = h.astype(BF16)
    hb_ref[...] = hb
    if split:
        two = _dot(hb, wr_ref[...])
        h_lo = (h - hb.astype(F32)).astype(BF16)
        logits = (two[:, :LANES] + two[:, LANES:]) + _dot(h_lo, wr_ref[:, :LANES]) + br_ref[...]
    else:
        logits = _dot_hi(h, wr_ref[...]) + br_ref[...]
    tm = logits.shape[0]
    lane = lax.broadcasted_iota(jnp.int32, (tm, LANES), 1).astype(F32)
    big = float(LANES)
    gmask = lane < N_GROUPS
    gl = jnp.where(gmask, logits, NEG)
    gmax = jnp.max(gl, axis=1, keepdims=True)
    gsum = jnp.sum(jnp.where(gmask, jnp.exp(gl - gmax), 0.0), axis=1, keepdims=True)
    g_val = 1.0 / gsum
    g_idx = jnp.min(jnp.where(gmask & (gl == gmax), lane, big), axis=1, keepdims=True)
    first = N_GROUPS + EXPERTS_PER_GROUP * g_idx
    emask = (lane >= first) & (lane < first + EXPERTS_PER_GROUP)
    el = jnp.where(emask, logits, NEG)
    v1 = jnp.max(el, axis=1, keepdims=True)
    i1 = jnp.min(jnp.where(emask & (el == v1), lane, big), axis=1, keepdims=True)
    el2 = jnp.where(lane == i1, NEG, el)
    v2 = jnp.max(el2, axis=1, keepdims=True)
    i2 = jnp.min(jnp.where(emask & (el2 == v2) & (lane != i1), lane, big), axis=1, keepdims=True)
    t = jnp.exp(v2 - v1)
    w1 = g_val / (1.0 + t)
    w2 = g_val * t / (1.0 + t)
    gates = jnp.where(lane == i1, w1, 0.0) + jnp.where(lane == i2, w2, 0.0)
    gate_ref[...] = gates
    hi = gates.astype(BF16).astype(F32)
    mid = (gates - hi).astype(BF16).astype(F32)
    lo = (gates - hi - mid).astype(BF16).astype(F32)
    route = (pltpu.roll(hi, LANES - N_GROUPS, 1) + pltpu.roll(mid, N_EXPERTS - N_GROUPS, 1)
             + pltpu.roll(lo, 2 * N_EXPERTS - N_GROUPS, 1))
    route = jnp.where(lane == ROUTE_MEMBER + g_idx, 1.0, route)
    route_ref[...] = route.astype(BF16)


def _tail(mix, x2d, wo_bf, g1, b1, wr, br_pad, tm, alpha):
    n = x2d.shape[0]
    row = lambda i: (i, 0)
    const = lambda i: (0, 0)
    return pl.pallas_call(
        functools.partial(_tail_kernel, alpha=alpha, split=wr.dtype == BF16),
        grid=(n // tm,),
        in_specs=[pl.BlockSpec((tm, D_MODEL), row), pl.BlockSpec((tm, D_MODEL), row),
                  pl.BlockSpec((D_MODEL, D_MODEL), const),
                  pl.BlockSpec((1, D_MODEL), const), pl.BlockSpec((1, D_MODEL), const),
                  pl.BlockSpec(wr.shape, const), pl.BlockSpec((1, LANES), const)],
        out_specs=[pl.BlockSpec((tm, D_MODEL), row), pl.BlockSpec((tm, D_MODEL), row),
                   pl.BlockSpec((tm, LANES), row), pl.BlockSpec((tm, LANES), row)],
        out_shape=[jax.ShapeDtypeStruct((n, D_MODEL), F32), jax.ShapeDtypeStruct((n, D_MODEL), BF16),
                   jax.ShapeDtypeStruct((n, LANES), F32), jax.ShapeDtypeStruct((n, LANES), BF16)],
        compiler_params=_cparams(("arbitrary",)),
        name="tail",
    )(mix, x2d, wo_bf, g1, b1, wr, br_pad)


def _swiglu_rows(x, gate, wg_ref, wu_ref, wd_ref):
    a = _dot(x, wg_ref[0])
    u = _dot(x, wu_ref[0])
    hmid = (a * jax.nn.sigmoid(a)) * u * gate
    return _dot(hmid.astype(BF16), wd_ref[0])


def _expert_specs(ix):
    return [pl.BlockSpec((1, D_MODEL, D_EXPERT), ix), pl.BlockSpec((1, D_MODEL, D_EXPERT), ix),
            pl.BlockSpec((1, D_EXPERT, D_MODEL), ix)]


def _moe_kernel(hb_ref, h_ref, gate_ref, wg_ref, wu_ref, wd_ref, g_ref, b_ref, y_ref, acc_ref, *, alpha):
    e = pl.program_id(1)

    @pl.when(e == 0)
    def _():
        acc_ref[...] = jnp.zeros(acc_ref.shape, F32)

    xb = hb_ref[...]
    lane = lax.broadcasted_iota(jnp.int32, (xb.shape[0], LANES), 1)
    gate = jnp.sum(jnp.where(lane == e + N_GROUPS, gate_ref[...], 0.0), axis=1, keepdims=True)
    acc_ref[...] += _swiglu_rows(xb, gate, wg_ref, wu_ref, wd_ref)

    @pl.when(e == N_EXPERTS - 1)
    def _():
        y_ref[...] = _layer_norm(alpha * h_ref[...] + acc_ref[...], g_ref[...], b_ref[...])


def _moe(hb, h, gates, wg_bf, wu_bf, wd_bf, g2, b2, tm, alpha):
    n = h.shape[0]
    row = lambda i, e: (i, 0)
    const = lambda i, e: (0, 0)
    return pl.pallas_call(
        functools.partial(_moe_kernel, alpha=alpha),
        grid=(n // tm, N_EXPERTS),
        in_specs=[pl.BlockSpec((tm, D_MODEL), row), pl.BlockSpec((tm, D_MODEL), row),
                  pl.BlockSpec((tm, LANES), row)] + _expert_specs(lambda i, e: (e, 0, 0))
                 + [pl.BlockSpec((1, D_MODEL), const), pl.BlockSpec((1, D_MODEL), const)],
        out_specs=pl.BlockSpec((tm, D_MODEL), row),
        out_shape=jax.ShapeDtypeStruct((n, D_MODEL), F32),
        scratch_shapes=[pltpu.VMEM((tm, D_MODEL), F32)],
        compiler_params=_cparams(("arbitrary", "arbitrary")),
        name="moe",
    )(hb, h, gates, wg_bf, wu_bf, wd_bf, g2, b2)


def _route_gate(route, lane, e):
    pick = (lane == e) | (lane == e + N_EXPERTS) | (lane == e + 2 * N_EXPERTS)
    return jnp.sum(jnp.where(pick, route, 0.0), axis=1, keepdims=True)


def _moe_grouped_kernel(hb_ref, h_ref, route_ref, wg_ref, wu_ref, wd_ref, g_ref, b_ref, y_ref,
                        acc_ref, rcol_ref, rrow_ref, mrow_ref, xg_ref, gsel_ref, yg_ref, cnt_ref,
                        *, alpha, cap):
    e = pl.program_id(1)
    grp = e // EXPERTS_PER_GROUP
    ts = hb_ref.shape[0]
    cap_pad = yg_ref.shape[0]
    lane = lax.broadcasted_iota(jnp.int32, (ts, LANES), 1)

    @pl.when(e == 0)
    def _():
        acc_ref[...] = jnp.zeros(acc_ref.shape, F32)
        t = lax.broadcasted_iota(jnp.int32, (LANES, LANES), 0)
        u = lax.broadcasted_iota(jnp.int32, (LANES, LANES), 1)
        lower = (u < t).astype(BF16)
        carry_c = jnp.zeros((1, LANES), F32)
        carry_r = jnp.zeros((GROUP_ROWS, 1), F32)
        for blk in range(ts // LANES):
            sl = slice(blk * LANES, (blk + 1) * LANES)
            memb = jnp.where((u >= ROUTE_MEMBER) & (u < ROUTE_MEMBER + N_GROUPS),
                             route_ref[sl, :].astype(F32), 0.0)
            rcol_ref[sl, :] = _dot(lower, memb.astype(BF16)) + carry_c
            carry_c = carry_c + jnp.sum(memb, axis=0, keepdims=True)
            mrow = memb.T[ROUTE_MEMBER:ROUTE_MEMBER + GROUP_ROWS, :]
            mrow_ref[:, sl] = mrow
            rrow_ref[:, sl] = _dot_nt(mrow.astype(BF16), lower) + carry_r
            carry_r = carry_r + jnp.sum(mrow, axis=1, keepdims=True)
        lane1 = lax.broadcasted_iota(jnp.int32, (1, LANES), 1)
        for gi in range(N_GROUPS):
            cnt_ref[gi] = jnp.sum(jnp.where(lane1 == ROUTE_MEMBER + gi, carry_c, 0.0)).astype(jnp.int32)

    small = cnt_ref[grp] <= cap

    @pl.when((e % EXPERTS_PER_GROUP == 0) & small)
    def _():
        pos = lax.broadcasted_iota(jnp.int32, (cap, ts), 0).astype(F32)
        rank = jnp.where(mrow_ref[pl.ds(grp, 1), :] > 0.5, rrow_ref[pl.ds(grp, 1), :], -1.0)
        onehot = jnp.where(rank == pos, 1.0, 0.0).astype(BF16)
        xg_ref[...] = _dot(onehot, hb_ref[...]).astype(BF16)
        gsel_ref[...] = _dot(onehot, route_ref[...])
        yg_ref[...] = jnp.zeros(yg_ref.shape, F32)

    @pl.when(small)
    def _():
        lane_c = lax.broadcasted_iota(jnp.int32, (cap, LANES), 1)
        gate = _route_gate(gsel_ref[...], lane_c, e)
        yg_ref[0:cap, :] += _swiglu_rows(xg_ref[...], gate, wg_ref, wu_ref, wd_ref)

    @pl.when(jnp.logical_not(small))
    def _():
        gate = _route_gate(route_ref[...].astype(F32), lane, e)
        acc_ref[...] += _swiglu_rows(hb_ref[...], gate, wg_ref, wu_ref, wd_ref)

    @pl.when((e % EXPERTS_PER_GROUP == EXPERTS_PER_GROUP - 1) & small)
    def _():
        mine = lane == ROUTE_MEMBER + grp
        rank = jnp.sum(jnp.where(mine, rcol_ref[...], 0.0), axis=1, keepdims=True)
        member = jnp.sum(jnp.where(mine, route_ref[...].astype(F32), 0.0), axis=1, keepdims=True)
        pos = lax.broadcasted_iota(jnp.int32, (ts, cap_pad), 1).astype(F32)
        rank = jnp.where(member > 0.5, rank, -1.0)
        onehot_t = jnp.where(rank == pos, 1.0, 0.0).astype(BF16)
        acc_ref[...] += _dot(onehot_t, yg_ref[...].astype(BF16))

    @pl.when(e == N_EXPERTS - 1)
    def _():
        y_ref[...] = _layer_norm(alpha * h_ref[...] + acc_ref[...], g_ref[...], b_ref[...])


def _moe_grouped(hb, h, route, wg_bf, wu_bf, wd_bf, g2, b2, ts, cap, alpha):
    n = h.shape[0]
    cap_pad = -(-cap // LANES) * LANES
    row = lambda i, e: (i, 0)
    const = lambda i, e: (0, 0)
    return pl.pallas_call(
        functools.partial(_moe_grouped_kernel, alpha=alpha, cap=cap),
        grid=(n // ts, N_EXPERTS),
        in_specs=[pl.BlockSpec((ts, D_MODEL), row), pl.BlockSpec((ts, D_MODEL), row),
                  pl.BlockSpec((ts, LANES), row)] + _expert_specs(lambda i, e: (e, 0, 0))
                 + [pl.BlockSpec((1, D_MODEL), const), pl.BlockSpec((1, D_MODEL), const)],
        out_specs=pl.BlockSpec((ts, D_MODEL), row),
        out_shape=jax.ShapeDtypeStruct((n, D_MODEL), F32),
        scratch_shapes=[pltpu.VMEM((ts, D_MODEL), F32),
                        pltpu.VMEM((ts, LANES), F32),
                        pltpu.VMEM((GROUP_ROWS, ts), F32),
                        pltpu.VMEM((GROUP_ROWS, ts), F32),
                        pltpu.VMEM((cap, D_MODEL), BF16),
                        pltpu.VMEM((cap, LANES), F32),
                        pltpu.VMEM((cap_pad, D_MODEL), F32),
                        pltpu.SMEM((N_GROUPS,), jnp.int32)],
        compiler_params=_cparams(("arbitrary", "arbitrary")),
        name="moe_grouped",
    )(hb, h, route, wg_bf, wu_bf, wd_bf, g2, b2)


def _rope_tables(pos):
    half = HEAD_DIM // 2
    inv = ROPE_THETA ** (-jnp.arange(half, dtype=F32) * 2.0 / HEAD_DIM)
    ang = pos.astype(F32)[:, None] * inv[None, :]
    cos, sin = jnp.cos(ang), jnp.sin(ang)
    reps = WIDTH // HEAD_DIM
    cos_t = jnp.tile(jnp.concatenate([cos, cos], axis=1), (1, reps))
    sin_t = jnp.tile(jnp.concatenate([-sin, sin], axis=1), (1, reps))
    return cos_t, sin_t


def kernel(x_prompt, x_sample, cache_diff_k, cache_diff_v, cache_fox_k, cache_fox_v, cache_fox_logf, page_table,
           w_in, b_forget, lambda_q1, lambda_k1, lambda_q2, lambda_k2, subln_g, w_out, ln1_g, ln1_b,
           w_router_group, b_router_group, w_router_expert, b_router_expert, w_gate, w_up, w_down, ln2_g, ln2_b):
    B, S, D = x_prompt.shape
    DB, T, _ = x_sample.shape
    depth = w_in.shape[0]
    assert depth == 1 and T == 1 and D == D_MODEL
    n_pool, page = cache_diff_k.shape[1], cache_diff_k.shape[2]
    n_pages = page_table.shape[1]
    past_len = n_pages * page
    assert page == LANES
    alpha = (2 * depth) ** 0.25
    l = 0
    lam_init = 0.8 - 0.6 * math.exp(-0.3 * l)

    w_pad = jnp.pad(w_in[l], ((0, 0), (0, 6 * WIDTH + LANES - w_in.shape[2]))).astype(BF16)
    bf_pad = jnp.pad(b_forget[l], (0, LANES - FOX_HEADS)).reshape(1, LANES)
    lamv = jnp.stack([lambda_q1[l], lambda_k1[l], lambda_q2[l], lambda_k2[l]])
    g1 = subln_g[l].reshape(1, LANES)
    wo_bf = w_out[l].astype(BF16)
    wr_pad = jnp.pad(jnp.concatenate([w_router_group[l], w_router_expert[l]], axis=1),
                     ((0, 0), (0, LANES - N_GROUPS - N_EXPERTS)))
    br_pad = jnp.pad(jnp.concatenate([b_router_group[l], b_router_expert[l]]),
                     (0, LANES - N_GROUPS - N_EXPERTS)).reshape(1, LANES)
    wg_bf, wu_bf, wd_bf = w_gate[l].astype(BF16), w_up[l].astype(BF16), w_down[l].astype(BF16)
    ln1g, ln1b = ln1_g[l].reshape(1, D), ln1_b[l].reshape(1, D)
    ln2g, ln2b = ln2_g[l].reshape(1, D), ln2_b[l].reshape(1, D)
    p_cos, p_sin = _rope_tables(jnp.arange(S, dtype=jnp.int32))
    s_cos, s_sin = _rope_tables(jnp.full((DB,), past_len, dtype=jnp.int32))

    tq, tp = ATTN_TILE, PROJ_TILE
    xp2 = x_prompt.reshape(B * S, D)
    (q_p, k_p, dk, dv, fk, fv, logf, vt_p) = _project(xp2, w_pad, p_cos, p_sin, bf_pad, tp, S // tp, True)
    logf_t = jnp.swapaxes(logf.reshape(B, S, FOX_HEADS), 1, 2).reshape(B * FOX_HEADS, S)
    c_t = _cumsum_rows(logf_t).reshape(B, FOX_HEADS, S)
    c = jnp.swapaxes(c_t, 1, 2)
    c_blk = jnp.swapaxes(c_t.reshape(B, FOX_HEADS, S // tq, tq), 1, 2)
    mix_p = _prompt_attention(q_p.reshape(B, S, 2 * WIDTH), k_p.reshape(B, S, 2 * WIDTH),
                              vt_p.reshape(B, S // tp, 2 * WIDTH, tp), c, c_blk, lamv, g1, tq, lam_init)
    wr_hi = wr_pad.astype(BF16)
    wr_split = jnp.concatenate([wr_hi, (wr_pad - wr_hi.astype(F32)).astype(BF16)], axis=1)
    h_p, hb_p, _, route_p = _tail(mix_p.reshape(B * S, D), xp2, wo_bf, ln1g, ln1b, wr_split, br_pad, TAIL_TILE, alpha)
    y_p = _moe_grouped(hb_p, h_p, route_p, wg_bf, wu_bf, wd_bf, ln2g, ln2b, MOE_TILE, MOE_GROUP_CAP, alpha)

    xs2 = x_sample.reshape(DB, D)
    (q_s, _, sdk, sdv, sfk, sfv, slogf) = _project(xs2, w_pad, s_cos, s_sin, bf_pad, DB, 1, False)
    sdq, sfq = q_s[:, :WIDTH], q_s[:, WIDTH:]
    diff_pool = lambda a: a[l].reshape(n_pool, page * DIFF_HEADS, 2 * HEAD_DIM)
    fox_pool = lambda a: jnp.transpose(a[l], (0, 2, 3, 1)).reshape(n_pool, FOX_HEADS * HEAD_DIM, page)
    lft = jnp.transpose(cache_fox_logf[l], (0, 2, 1)).reshape(n_pool * FOX_HEADS, page)
    suffix = _page_suffix(lft, 4096)
    rep = lambda a: jnp.repeat(a.astype(F32).reshape(DB, DIFF_HEADS, 2 * HEAD_DIM), 2, axis=1)
    v3 = lambda a: a.astype(F32).reshape(DB, 1, WIDTH)
    dec_args = [rep(sdq), v3(sfq), rep(sdk), rep(sdv), v3(sfk), v3(sfv), slogf.reshape(DB, FOX_HEADS, 1), lamv, g1,
                diff_pool(cache_diff_k), diff_pool(cache_diff_v), fox_pool(cache_fox_k), fox_pool(cache_fox_v),
                suffix.reshape(n_pool, FOX_HEADS, page)]
    od_s, of_s = _decode_attention(page_table, dec_args, DECODE_PAGES, lam_init)
    mix_s = jnp.concatenate([od_s[:, 0::2, :].reshape(DB, WIDTH), of_s.reshape(DB, WIDTH)], axis=1)
    h_s, hb_s, gates_s, _ = _tail(mix_s, xs2, wo_bf, ln1g, ln1b, wr_pad, br_pad, DB, alpha)
    y_s = _moe(hb_s, h_s, gates_s, wg_bf, wu_bf, wd_bf, ln2g, ln2b, DB, alpha)

    fox_out = lambda a: jnp.transpose(a.reshape(B, FOX_HEADS, HEAD_DIM, S), (0, 3, 1, 2))[None]
    return (y_p.reshape(B, S, D), y_s.reshape(DB, T, D),
            dk.reshape(1, B, S, DIFF_HEADS, 2 * HEAD_DIM), dv.reshape(1, B, S, DIFF_HEADS, 2 * HEAD_DIM),
            fox_out(fk), fox_out(fv),
            logf.reshape(1, B, S, FOX_HEADS),
            sdk.reshape(1, DB, T, DIFF_HEADS, 2 * HEAD_DIM), sdv.reshape(1, DB, T, DIFF_HEADS, 2 * HEAD_DIM),
            sfk.reshape(1, DB, T, FOX_HEADS, HEAD_DIM), sfv.reshape(1, DB, T, FOX_HEADS, HEAD_DIM),
            slogf.reshape(1, DB, T, FOX_HEADS))
```
